```python
import math
import jax, jax.numpy as jnp
from jax import lax
import numpy as np

D_MODEL = 1024
BATCH = 16
SEQ = 2048
DEPTH = 2
DEC_BATCH = 128
DEC_SEQ = 8
PAST_LEN = 16384
PAGE_SIZE = 128

N_HEADS = 8
QK_NOPE = 128
QK_ROPE = 64
V_DIM = 128
Q_LORA = 512
KV_LORA = 256
ROPE_THETA = 10000.0
Q_BLOCK = 128
ATTN_GATE = N_HEADS * V_DIM
ATTN_IN = Q_LORA + KV_LORA + QK_ROPE + ATTN_GATE
SM_SCALE = 1.0 / math.sqrt(QK_NOPE + QK_ROPE)
NEG_INF = -1e30
LRU_WIDTH = D_MODEL
LRU_BLOCKS = 4
LRU_BW = LRU_WIDTH // LRU_BLOCKS
CONV_W = 4
LRU_C = 8.0
NORM_EPS = 1e-6
N_ATTN = (DEPTH + 1) // 2
N_REC = DEPTH // 2

kernel_name = "hybrid_mla_rglru_adaln_decode_step"


def rmsnorm(x, g):
    xf = x.astype(jnp.float32)
    y = xf * lax.rsqrt(jnp.mean(xf * xf, axis=-1, keepdims=True) + NORM_EPS)
    return (y * g.astype(jnp.float32)).astype(x.dtype)


def rope(x, pos):
    half = x.shape[-1] // 2
    inv = ROPE_THETA ** (-jnp.arange(half, dtype=jnp.float32) / half)
    ang = pos.astype(jnp.float32)[:, None] * inv[None, :]
    ang = ang.reshape((1, ang.shape[0]) + (1,) * (x.ndim - 3) + (half,))
    cos, sin = jnp.cos(ang), jnp.sin(ang)
    xf = x.astype(jnp.float32)
    x1, x2 = xf[..., :half], xf[..., half:]
    return jnp.concatenate([x1 * cos - x2 * sin, x1 * sin + x2 * cos], axis=-1).astype(x.dtype)


def adaln(x, c, g, w, b):
    mod = c @ w + b
    shift, scale, gate = jnp.split(mod, 3, axis=-1)
    h = rmsnorm(x, g) * (1 + scale[:, None, :]) + shift[:, None, :]
    return h, gate[:, None, :]


def mla_project(h, pos, w_in, g_q, w_qb, g_kv, w_uk):
    B, T, _ = h.shape
    u = h @ w_in
    q_a, kv_a, k_pe, z = jnp.split(u, [Q_LORA, Q_LORA + KV_LORA, Q_LORA + KV_LORA + QK_ROPE], axis=-1)
    q = (rmsnorm(q_a, g_q) @ w_qb).reshape(B, T, N_HEADS, QK_NOPE + QK_ROPE)
    q_nope, q_pe = q[..., :QK_NOPE], q[..., QK_NOPE:]
    q_lat = jnp.einsum('bthn,chn->bthc', q_nope, w_uk)
    ckv = rmsnorm(kv_a, g_kv)
    kpe = rope(k_pe, pos)
    return q_lat, rope(q_pe, pos), ckv, kpe, z


def mla_scores(q_lat, q_pe, ckv, kpe):
    s = jnp.einsum('bthc,blc->bhtl', q_lat, ckv, preferred_element_type=jnp.float32)
    s = s + jnp.einsum('bthr,blr->bhtl', q_pe, kpe, preferred_element_type=jnp.float32)
    return s * SM_SCALE


def attend_prompt(q_lat, q_pe, ckv, kpe):
    B, T, H, C = q_lat.shape
    qb = min(Q_BLOCK, T)
    nb = T // qb
    ql = q_lat.reshape(B, nb, qb, H, C).transpose(1, 0, 2, 3, 4)
    qp = q_pe.reshape(B, nb, qb, H, QK_ROPE).transpose(1, 0, 2, 3, 4)
    k_pos = jnp.arange(T)

    def block(args):
        ql_b, qp_b, start = args
        q_pos = start + jnp.arange(qb)
        s = mla_scores(ql_b, qp_b, ckv, kpe)
        s = jnp.where(k_pos[None, :] <= q_pos[:, None], s, NEG_INF)
        p = jax.nn.softmax(s, axis=-1)
        return jnp.einsum('bhtl,blc->bthc', p.astype(ckv.dtype), ckv)

    o = lax.map(block, (ql, qp, jnp.arange(nb) * qb))
    return o.transpose(1, 0, 2, 3, 4).reshape(B, T, H, C)


def attend_sample(q_lat, q_pe, ckv_past, kpe_past, ckv_new, kpe_new):
    T = q_lat.shape[1]
    P = ckv_past.shape[1]
    s_past = mla_scores(q_lat, q_pe, ckv_past, kpe_past)
    s_new = mla_scores(q_lat, q_pe, ckv_new, kpe_new)
    causal = jnp.arange(T)[None, :] <= jnp.arange(T)[:, None]
    s_new = jnp.where(causal, s_new, NEG_INF)
    p = jax.nn.softmax(jnp.concatenate([s_past, s_new], axis=-1), axis=-1)
    p_past, p_new = p[..., :P], p[..., P:]
    return (jnp.einsum('bhtl,blc->bthc', p_past.astype(ckv_past.dtype), ckv_past)
            + jnp.einsum('bhtl,blc->bthc', p_new.astype(ckv_new.dtype), ckv_new))


def mla_out(o_lat, z, w_uv, w_o):
    B, T = o_lat.shape[:2]
    v = jnp.einsum('bthc,chv->bthv', o_lat, w_uv).reshape(B, T, ATTN_GATE)
    return (v * jax.nn.silu(z)) @ w_o


def rglru_mixer(h, conv_buf, h0, w_in, conv_w, conv_b, w_a, b_a, w_x, b_x, lam, w_o):
    B, T, _ = h.shape
    xb, z = jnp.split(h @ w_in, 2, axis=-1)
    xpad = jnp.concatenate([conv_buf.astype(xb.dtype), xb], axis=1)
    new_conv = xpad[:, -(CONV_W - 1):]
    xc = conv_b + xpad[:, 0:T] * conv_w[0]
    for k in range(1, CONV_W):
        xc = xc + xpad[:, k:k + T] * conv_w[k]
    xr = xc.reshape(B, T, LRU_BLOCKS, LRU_BW)
    r = jax.nn.sigmoid((jnp.einsum('btnw,nwv->btnv', xr, w_a) + b_a).astype(jnp.float32)).reshape(B, T, LRU_WIDTH)
    i = jax.nn.sigmoid((jnp.einsum('btnw,nwv->btnv', xr, w_x) + b_x).astype(jnp.float32)).reshape(B, T, LRU_WIDTH)
    log_a = -LRU_C * r * jax.nn.softplus(-lam.astype(jnp.float32))
    a = jnp.exp(log_a)
    b = jnp.sqrt(-jnp.expm1(2.0 * log_a)) * (i * xc.astype(jnp.float32))

    def step(hc, ab):
        hc = ab[0] * hc + ab[1]
        return hc, hc

    h_last, hs = lax.scan(step, h0.astype(jnp.float32), (a.transpose(1, 0, 2), b.transpose(1, 0, 2)))
    y = hs.transpose(1, 0, 2).astype(h.dtype)
    return (y * jax.nn.silu(z)) @ w_o, new_conv, h_last


def setup_inputs(seed: int = 0) -> dict:
    key = jax.random.key(seed)
    ks = iter(jax.random.split(key, 48))
    f32 = jnp.float32

    def nrm(shape, s):
        return jax.random.normal(next(ks), shape, f32) * s

    def gain(shape):
        return 1.0 + nrm(shape, 0.01)

    n_pages = PAST_LEN // PAGE_SIZE
    n_used = DEC_BATCH * n_pages
    n_phys = n_used + max(1, n_used // 4)
    page_table = jax.random.permutation(next(ks), n_phys)[:n_used].reshape(DEC_BATCH, n_pages).astype(jnp.int32)

    a0 = jax.random.uniform(next(ks), (N_REC, LRU_WIDTH), f32, minval=0.9, maxval=0.999)
    lam = jnp.log(a0) - jnp.log1p(-a0)

    return {
        "x_prompt": nrm((BATCH, SEQ, D_MODEL), 1.0),
        "x_sample": nrm((DEC_BATCH, DEC_SEQ, D_MODEL), 1.0),
        "c_prompt": nrm((BATCH, D_MODEL), 1.0),
        "c_sample": nrm((DEC_BATCH, D_MODEL), 1.0),
        "cache_ckv": nrm((N_ATTN, n_phys, PAGE_SIZE, KV_LORA), 1.0),
        "cache_kpe": nrm((N_ATTN, n_phys, PAGE_SIZE, QK_ROPE), 1.0),
        "page_table": page_table,
        "state_conv": nrm((N_REC, DEC_BATCH, CONV_W - 1, LRU_WIDTH), 1.0),
        "state_h": nrm((N_REC, DEC_BATCH, LRU_WIDTH), 0.5),
        "norm_g": gain((DEPTH, D_MODEL)),
        "w_ada": nrm((DEPTH, D_MODEL, 3 * D_MODEL), 0.5 * D_MODEL ** -0.5),
        "b_ada": nrm((DEPTH, 3 * D_MODEL), 0.01),
        "final_g": gain((D_MODEL,)),
        "a_w_in": nrm((N_ATTN, D_MODEL, ATTN_IN), D_MODEL ** -0.5),
        "a_g_q": gain((N_ATTN, Q_LORA)),
        "a_w_qb": nrm((N_ATTN, Q_LORA, N_HEADS * (QK_NOPE + QK_ROPE)), Q_LORA ** -0.5),
        "a_g_kv": gain((N_ATTN, KV_LORA)),
        "a_w_uk": nrm((N_ATTN, KV_LORA, N_HEADS, QK_NOPE), KV_LORA ** -0.5),
        "a_w_uv": nrm((N_ATTN, KV_LORA, N_HEADS, V_DIM), KV_LORA ** -0.5),
        "a_w_o": nrm((N_ATTN, ATTN_GATE, D_MODEL), ATTN_GATE ** -0.5),
        "r_w_in": nrm((N_REC, D_MODEL, 2 * LRU_WIDTH), D_MODEL ** -0.5),
        "r_conv_w": nrm((N_REC, CONV_W, LRU_WIDTH), CONV_W ** -0.5),
        "r_conv_b": nrm((N_REC, LRU_WIDTH), 0.01),
        "r_w_a": nrm((N_REC, LRU_BLOCKS, LRU_BW, LRU_BW), LRU_BW ** -0.5),
        "r_b_a": nrm((N_REC, LRU_BLOCKS, LRU_BW), 0.01),
        "r_w_x": nrm((N_REC, LRU_BLOCKS, LRU_BW, LRU_BW), LRU_BW ** -0.5),
        "r_b_x": nrm((N_REC, LRU_BLOCKS, LRU_BW), 0.01),
        "r_lam": lam,
        "r_w_o": nrm((N_REC, LRU_WIDTH, D_MODEL), LRU_WIDTH ** -0.5),
    }


def reference(x_prompt, x_sample, c_prompt, c_sample, cache_ckv, cache_kpe, page_table, state_conv, state_h,
              norm_g, w_ada, b_ada, final_g,
              a_w_in, a_g_q, a_w_qb, a_g_kv, a_w_uk, a_w_uv, a_w_o,
              r_w_in, r_conv_w, r_conv_b, r_w_a, r_b_a, r_w_x, r_b_x, r_lam, r_w_o):
    Bp, Tp, _ = x_prompt.shape
    Bs, Ts, _ = x_sample.shape
    pos_p = jnp.arange(Tp)
    pos_s = PAST_LEN + jnp.arange(Ts)
    xp, xs = x_prompt, x_sample
    ckv_p_l, kpe_p_l, ckv_s_l, kpe_s_l = [], [], [], []
    conv_p_l, conv_s_l, h_p_l, h_s_l = [], [], [], []

    for i in range(DEPTH):
        hp, gate_p = adaln(xp, c_prompt, norm_g[i], w_ada[i], b_ada[i])
        hs, gate_s = adaln(xs, c_sample, norm_g[i], w_ada[i], b_ada[i])
        j = i // 2
        if i % 2 == 0:
            ql_p, qp_p, ckv_p, kpe_p, z_p = mla_project(hp, pos_p, a_w_in[j], a_g_q[j], a_w_qb[j], a_g_kv[j], a_w_uk[j])
            out_p = mla_out(attend_prompt(ql_p, qp_p, ckv_p, kpe_p), z_p, a_w_uv[j], a_w_o[j])

            ql_s, qp_s, ckv_s, kpe_s, z_s = mla_project(hs, pos_s, a_w_in[j], a_g_q[j], a_w_qb[j], a_g_kv[j], a_w_uk[j])
            ckv_past = cache_ckv[j, page_table].reshape(Bs, -1, KV_LORA)
            kpe_past = cache_kpe[j, page_table].reshape(Bs, -1, QK_ROPE)
            o_s = attend_sample(ql_s, qp_s, ckv_past, kpe_past, ckv_s, kpe_s)
            out_s = mla_out(o_s, z_s, a_w_uv[j], a_w_o[j])

            ckv_p_l.append(ckv_p); kpe_p_l.append(kpe_p)
            ckv_s_l.append(ckv_s); kpe_s_l.append(kpe_s)
        else:
            zero_conv = jnp.zeros((Bp, CONV_W - 1, LRU_WIDTH), xp.dtype)
            zero_h = jnp.zeros((Bp, LRU_WIDTH), jnp.float32)
            out_p, conv_p, h_p = rglru_mixer(hp, zero_conv, zero_h, r_w_in[j], r_conv_w[j], r_conv_b[j],
                                             r_w_a[j], r_b_a[j], r_w_x[j], r_b_x[j], r_lam[j], r_w_o[j])
            out_s, conv_s, h_s = rglru_mixer(hs, state_conv[j], state_h[j], r_w_in[j], r_conv_w[j], r_conv_b[j],
                                             r_w_a[j], r_b_a[j], r_w_x[j], r_b_x[j], r_lam[j], r_w_o[j])
            conv_p_l.append(conv_p); conv_s_l.append(conv_s)
            h_p_l.append(h_p); h_s_l.append(h_s)
        xp = xp + gate_p * out_p
        xs = xs + gate_s * out_s

    y_prompt = rmsnorm(xp, final_g)
    y_sample = rmsnorm(xs, final_g)
    new_ckv_prompt = jnp.stack(ckv_p_l)
    new_kpe_prompt = jnp.stack(kpe_p_l)
    new_ckv_sample = jnp.stack(ckv_s_l)
    new_kpe_sample = jnp.stack(kpe_s_l)
    new_conv_prompt = jnp.stack(conv_p_l)
    new_conv_sample = jnp.stack(conv_s_l)
    new_h_prompt = jnp.stack(h_p_l)
    new_h_sample = jnp.stack(h_s_l)
    return (y_prompt, y_sample, new_ckv_prompt, new_kpe_prompt, new_ckv_sample, new_kpe_sample,
            new_conv_prompt, new_conv_sample, new_h_prompt, new_h_sample)
```

```python
import functools
import math

import jax
import jax.numpy as jnp
from jax import lax
from jax.experimental import pallas as pl
from jax.experimental.pallas import tpu as pltpu

F32 = jnp.float32
BF16 = jnp.bfloat16

N_HEADS = 8
QK_NOPE = 128
QK_ROPE = 64
V_DIM = 128
Q_LORA = 512
KV_LORA = 256
ROPE_THETA = 10000.0
SM_SCALE = 1.0 / math.sqrt(QK_NOPE + QK_ROPE)
Q_SCALE = SM_SCALE * math.log2(math.e)
NEG_INF = -1e30
LRU_BLOCKS = 4
CONV_W = 4
LRU_C = 8.0
NORM_EPS = 1e-6
PAGE_SIZE = 128

VMEM_LIMIT_BYTES = 56 * 1024 * 1024
SUBLANES = 8
ROW_TILE = 256


def _dot(a, b):
    return jnp.dot(a, b, preferred_element_type=F32)


def _dot_nt(a, b):
    return lax.dot_general(a, b, (((1,), (1,)), ((), ())), preferred_element_type=F32)


def _rms(x, g):
    return x * lax.rsqrt(jnp.mean(x * x, axis=-1, keepdims=True) + NORM_EPS) * g


def _sigmoid(x):
    return 0.5 * jnp.tanh(0.5 * x) + 0.5


def _mod_kernel(c_ref, w_ref, b_ref, o_ref):
    o_ref[0] = _dot(c_ref[...].astype(BF16), w_ref[0].astype(BF16)) + b_ref[0]


def _modulation(c_all, w_ada, b_ada):
    depth, d, d3 = w_ada.shape
    n = c_all.shape[0]
    return pl.pallas_call(
        _mod_kernel,
        out_shape=jax.ShapeDtypeStruct((depth, n, d3), F32),
        grid=(depth, d3 // d),
        in_specs=[
            pl.BlockSpec((n, d), lambda i, j: (0, 0)),
            pl.BlockSpec((1, d, d), lambda i, j: (i, 0, j)),
            pl.BlockSpec((1, 1, d), lambda i, j: (i, 0, j)),
        ],
        out_specs=pl.BlockSpec((1, n, d), lambda i, j: (i, 0, j)),
        compiler_params=pltpu.CompilerParams(
            dimension_semantics=("arbitrary", "arbitrary"), vmem_limit_bytes=VMEM_LIMIT_BYTES),
        name="mod",
    )(c_all, w_ada, b_ada.reshape(depth, 1, d3))


def _front_kernel(x_ref, shift_ref, scale_ref, cos_ref, sin_ref, g_ref, w_in_ref, gq_ref, w_qb_ref,
                  gkv_ref, w_uk_ref,
                  qlat_ref, qpe_ref, ckv_ref, kpe_ref, ckvb_ref, kpeb_ref, zg_ref, *, nb, tt):
    rows = nb * tt
    d = x_ref.shape[-1]
    nope_w = N_HEADS * QK_NOPE
    pe_w = N_HEADS * QK_ROPE

    x = x_ref[...]
    h = _rms(x, g_ref[...]) * (1.0 + scale_ref[...]) + shift_ref[...]
    hb = h.reshape(rows, d).astype(BF16)

    c0, c1, c2, c3 = Q_LORA, Q_LORA + KV_LORA, Q_LORA + KV_LORA + d, Q_LORA + KV_LORA + d + 2 * QK_ROPE
    u_q = _dot(hb, w_in_ref[:, 0:c0])
    u_kv = _dot(hb, w_in_ref[:, c0:c1])
    z = _dot(hb, w_in_ref[:, c1:c2])
    u_kpe = _dot(hb, w_in_ref[:, c2:c3])

    zg_ref[...] = (z * _sigmoid(z)).reshape(nb, tt, d).astype(zg_ref.dtype)

    ckv = _rms(u_kv, gkv_ref[...]).reshape(nb, tt, KV_LORA)
    ckv_ref[...] = ckv
    ckvb_ref[...] = ckv.astype(ckvb_ref.dtype)

    cos = cos_ref[...]
    sin = sin_ref[...]
    kr = _rope_rows(u_kpe, cos[:, 0:2 * QK_ROPE], sin[:, 0:2 * QK_ROPE], nb, tt)[:, :, 0:QK_ROPE]
    kpe_ref[...] = kr
    kpeb_ref[...] = kr.astype(kpeb_ref.dtype)

    qn = _rms(u_q, gq_ref[...]).astype(BF16)
    q = _dot(qn, w_qb_ref[...])
    for hd in range(N_HEADS):
        qh = q[:, hd * QK_NOPE:(hd + 1) * QK_NOPE].astype(BF16)
        ql = _dot(qh, w_uk_ref[hd]) * Q_SCALE
        qlat_ref[:, hd] = ql.reshape(nb, tt, KV_LORA).astype(qlat_ref.dtype)
    qr = _rope_rows(q[:, nope_w:nope_w + pe_w], cos, sin, nb, tt) * Q_SCALE
    for hd in range(N_HEADS):
        qpe_ref[:, hd] = qr[:, :, hd * QK_ROPE:(hd + 1) * QK_ROPE].astype(qpe_ref.dtype)


def _rope_rows(x, cos, sin_signed, nb, tt):
    width = x.shape[-1]
    lane = lax.broadcasted_iota(jnp.int32, x.shape, 1)
    first_half = (lane & (QK_ROPE - 1)) < (QK_ROPE // 2)
    partner = jnp.where(first_half,
                        pltpu.roll(x, width - QK_ROPE // 2, 1),
                        pltpu.roll(x, QK_ROPE // 2, 1))
    x3 = x.reshape(nb, tt, width)
    p3 = partner.reshape(nb, tt, width)
    return x3 * cos[None] + p3 * sin_signed[None]


def _front(x, mod3, cos_t, sin_t, g, w_in_p, g_q, w_qb_p, g_kv, w_uk_p, *, nb, tt, q_dtype):
    bseq, t, d = x.shape
    in_w = w_in_p.shape[1]
    pe_w = N_HEADS * QK_ROPE
    grid = (bseq // nb, t // tt)
    full = lambda shape: pl.BlockSpec(shape, lambda i, j: (0,) * len(shape))
    out_shape = (
        jax.ShapeDtypeStruct((bseq, N_HEADS, t, KV_LORA), q_dtype),
        jax.ShapeDtypeStruct((bseq, N_HEADS, t, QK_ROPE), q_dtype),
        jax.ShapeDtypeStruct((bseq, t, KV_LORA), F32),
        jax.ShapeDtypeStruct((bseq, t, QK_ROPE), F32),
        jax.ShapeDtypeStruct((bseq, t, KV_LORA), BF16),
        jax.ShapeDtypeStruct((bseq, t, QK_ROPE), BF16),
        jax.ShapeDtypeStruct((bseq, t, d), BF16),
    )
    out_specs = (
        pl.BlockSpec((nb, N_HEADS, tt, KV_LORA), lambda i, j: (i, 0, j, 0)),
        pl.BlockSpec((nb, N_HEADS, tt, QK_ROPE), lambda i, j: (i, 0, j, 0)),
        pl.BlockSpec((nb, tt, KV_LORA), lambda i, j: (i, j, 0)),
        pl.BlockSpec((nb, tt, QK_ROPE), lambda i, j: (i, j, 0)),
        pl.BlockSpec((nb, tt, KV_LORA), lambda i, j: (i, j, 0)),
        pl.BlockSpec((nb, tt, QK_ROPE), lambda i, j: (i, j, 0)),
        pl.BlockSpec((nb, tt, d), lambda i, j: (i, j, 0)),
    )
    in_specs = [
        pl.BlockSpec((nb, tt, d), lambda i, j: (i, j, 0)),
        pl.BlockSpec((nb, 1, d), lambda i, j: (i, 0, 0)),
        pl.BlockSpec((nb, 1, d), lambda i, j: (i, 0, 1)),
        pl.BlockSpec((tt, pe_w), lambda i, j: (j, 0)),
        pl.BlockSpec((tt, pe_w), lambda i, j: (j, 0)),
        full((1, d)),
        full((d, in_w)),
        full((1, Q_LORA)),
        full(w_qb_p.shape),
        full((1, KV_LORA)),
        full(w_uk_p.shape),
    ]
    return pl.pallas_call(
        functools.partial(_front_kernel, nb=nb, tt=tt),
        out_shape=out_shape,
        grid=grid,
        in_specs=in_specs,
        out_specs=out_specs,
        compiler_params=pltpu.CompilerParams(
            dimension_semantics=("arbitrary", "arbitrary"), vmem_limit_bytes=VMEM_LIMIT_BYTES),
        name="front",
    )(x, mod3, mod3, cos_t, sin_t, g, w_in_p, g_q, w_qb_p, g_kv, w_uk_p)


def _attn_prompt_kernel(ql_ref, qp_ref, k_ref, kp_ref, o_ref, m_scr, l_scr, acc_scr, *, tq):
    qi = pl.program_id(1)
    m_rows = N_HEADS * tq
    ql = ql_ref[0].reshape(m_rows, KV_LORA)
    qp = qp_ref[0].reshape(m_rows, QK_ROPE)

    m_scr[...] = jnp.full(m_scr.shape, NEG_INF, F32)
    l_scr[...] = jnp.zeros(l_scr.shape, F32)
    acc_scr[...] = jnp.zeros(acc_scr.shape, F32)

    def step(j, masked):
        start = pl.multiple_of(j * tq, tq)
        k = k_ref[0, pl.ds(start, tq), :]
        kp = kp_ref[0, pl.ds(start, tq), :]
        s = _dot_nt(ql, k) + _dot_nt(qp, kp)
        if masked:
            s3 = s.reshape(N_HEADS, tq, tq)
            qpos = lax.broadcasted_iota(jnp.int32, s3.shape, 1)
            kpos = lax.broadcasted_iota(jnp.int32, s3.shape, 2)
            s = jnp.where(kpos <= qpos, s3, NEG_INF).reshape(m_rows, tq)
        m_prev = m_scr[...]
        m_new = jnp.maximum(m_prev, jnp.max(s, axis=1, keepdims=True))
        alpha = jnp.exp2(m_prev - m_new)
        p = jnp.exp2(s - m_new)
        l_scr[...] = alpha * l_scr[...] + jnp.sum(p, axis=1, keepdims=True)
        acc_scr[...] = alpha * acc_scr[...] + _dot(p.astype(BF16), k)
        m_scr[...] = m_new

    def body(j, carry):
        step(j, False)
        return carry

    lax.fori_loop(0, qi, body, 0)
    step(qi, True)

    out = acc_scr[...] / l_scr[...]
    o_ref[0] = out.reshape(N_HEADS, tq, KV_LORA).astype(o_ref.dtype)


def _attn_prompt(qlat, qpe, ckvb, kpeb, *, tq):
    b, _, t, _ = qlat.shape
    m_rows = N_HEADS * tq
    return pl.pallas_call(
        functools.partial(_attn_prompt_kernel, tq=tq),
        out_shape=jax.ShapeDtypeStruct((b, N_HEADS, t, KV_LORA), BF16),
        grid=(b, t // tq),
        in_specs=[
            pl.BlockSpec((1, N_HEADS, tq, KV_LORA), lambda i, j: (i, 0, j, 0)),
            pl.BlockSpec((1, N_HEADS, tq, QK_ROPE), lambda i, j: (i, 0, j, 0)),
            pl.BlockSpec((1, t, KV_LORA), lambda i, j: (i, 0, 0)),
            pl.BlockSpec((1, t, QK_ROPE), lambda i, j: (i, 0, 0)),
        ],
        out_specs=pl.BlockSpec((1, N_HEADS, tq, KV_LORA), lambda i, j: (i, 0, j, 0)),
        scratch_shapes=[
            pltpu.VMEM((m_rows, 1), F32),
            pltpu.VMEM((m_rows, 1), F32),
            pltpu.VMEM((m_rows, KV_LORA), F32),
        ],
        compiler_params=pltpu.CompilerParams(
            dimension_semantics=("arbitrary", "arbitrary"), vmem_limit_bytes=VMEM_LIMIT_BYTES),
        name="attn_prompt",
    )(qlat, qpe, ckvb, kpeb)


SAMPLE_CHUNK_PAGES = 16
SAMPLE_SLOTS = 4


def _attn_sample_kernel(pt_ref, ql_ref, qp_ref, kn_ref, kpn_ref, cache_k, cache_p, o_ref,
                        kbuf, pbuf, sems, *, n_chunks, ts):
    b = pl.program_id(0)
    nseq = pl.num_programs(0)
    cp = SAMPLE_CHUNK_PAGES
    rows = N_HEADS * ts

    def page_copies(seq, chunk, slot, page):
        phys = pt_ref[seq, chunk * cp + page]
        dst = pl.ds(page * PAGE_SIZE, PAGE_SIZE)
        return (
            pltpu.make_async_copy(cache_k.at[phys], kbuf.at[slot, dst], sems.at[slot, 0]),
            pltpu.make_async_copy(cache_p.at[phys], pbuf.at[slot, dst], sems.at[slot, 1]),
        )

    def start_chunk(seq, chunk, slot):
        for page in range(cp):
            for c in page_copies(seq, chunk, slot, page):
                c.start()

    def wait_chunk(seq, chunk, slot):
        for page in range(cp):
            for c in page_copies(seq, chunk, slot, page):
                c.wait()

    ahead = SAMPLE_SLOTS - 1

    @pl.when(b == 0)
    def _():
        for c in range(ahead):
            start_chunk(0, c, c % SAMPLE_SLOTS)

    ql = ql_ref[0].reshape(rows, KV_LORA).astype(BF16)
    qp = qp_ref[0].reshape(rows, QK_ROPE).astype(BF16)

    m = jnp.full((rows, 1), NEG_INF, F32)
    l = jnp.zeros((rows, 1), F32)
    acc = jnp.zeros((rows, KV_LORA), F32)

    def update(m, l, acc, s, v):
        m_new = jnp.maximum(m, jnp.max(s, axis=1, keepdims=True))
        alpha = jnp.exp2(m - m_new)
        p = jnp.exp2(s - m_new)
        l = alpha * l + jnp.sum(p, axis=1, keepdims=True)
        acc = alpha * acc + _dot(p.astype(BF16), v)
        return m_new, l, acc

    for c in range(n_chunks):
        slot = c % SAMPLE_SLOTS
        nxt = c + ahead
        nxt_slot = nxt % SAMPLE_SLOTS
        if nxt < n_chunks:
            start_chunk(b, nxt, nxt_slot)
        else:
            @pl.when(b + 1 < nseq)
            def _(nxt=nxt, nxt_slot=nxt_slot):
                start_chunk(b + 1, nxt - n_chunks, nxt_slot)
        wait_chunk(b, c, slot)
        k = kbuf[slot].astype(BF16)
        kp = pbuf[slot].astype(BF16)
        s = _dot_nt(ql, k) + _dot_nt(qp, kp)
        m, l, acc = update(m, l, acc, s, k)

    kn = kn_ref[0].astype(BF16)
    kpn = kpn_ref[0].astype(BF16)
    s = _dot_nt(ql, kn) + _dot_nt(qp, kpn)
    s3 = s.reshape(N_HEADS, ts, ts)
    qpos = lax.broadcasted_iota(jnp.int32, s3.shape, 1)
    kpos = lax.broadcasted_iota(jnp.int32, s3.shape, 2)
    s = jnp.where(kpos <= qpos, s3, NEG_INF).reshape(rows, ts)
    m, l, acc = update(m, l, acc, s, kn)

    o_ref[0] = (acc / l).reshape(N_HEADS, ts, KV_LORA)


def _attn_sample(page_table, qlat, qpe, ckv_new, kpe_new, cache_ckv, cache_kpe):
    bs, _, ts, _ = qlat.shape
    n_pages = page_table.shape[1]
    cp = SAMPLE_CHUNK_PAGES
    assert n_pages % cp == 0
    n_chunks = n_pages // cp
    assert n_chunks % SAMPLE_SLOTS == 0 and n_chunks >= SAMPLE_SLOTS
    grid_spec = pltpu.PrefetchScalarGridSpec(
        num_scalar_prefetch=1,
        grid=(bs,),
        in_specs=[
            pl.BlockSpec((1, N_HEADS, ts, KV_LORA), lambda i, pt: (i, 0, 0, 0)),
            pl.BlockSpec((1, N_HEADS, ts, QK_ROPE), lambda i, pt: (i, 0, 0, 0)),
            pl.BlockSpec((1, ts, KV_LORA), lambda i, pt: (i, 0, 0)),
            pl.BlockSpec((1, ts, QK_ROPE), lambda i, pt: (i, 0, 0)),
            pl.BlockSpec(memory_space=pl.ANY),
            pl.BlockSpec(memory_space=pl.ANY),
        ],
        out_specs=pl.BlockSpec((1, N_HEADS, ts, KV_LORA), lambda i, pt: (i, 0, 0, 0)),
        scratch_shapes=[
            pltpu.VMEM((SAMPLE_SLOTS, cp * PAGE_SIZE, KV_LORA), F32),
            pltpu.VMEM((SAMPLE_SLOTS, cp * PAGE_SIZE, QK_ROPE), F32),
            pltpu.SemaphoreType.DMA((SAMPLE_SLOTS, 2)),
        ],
    )
    return pl.pallas_call(
        functools.partial(_attn_sample_kernel, n_chunks=n_chunks, ts=ts),
        out_shape=jax.ShapeDtypeStruct((bs, N_HEADS, ts, KV_LORA), F32),
        grid_spec=grid_spec,
        compiler_params=pltpu.CompilerParams(
            dimension_semantics=("arbitrary",), vmem_limit_bytes=VMEM_LIMIT_BYTES),
        name="attn_sample",
    )(page_table, qlat, qpe, ckv_new, kpe_new, cache_ckv, cache_kpe)


def _back_kernel(o_ref, zg_ref, x_ref, gate0_ref, shift_ref, scale_ref, gate1_ref, conv0_ref, h0_ref,
                 w_uv_ref, w_o_ref, g1_ref, w_in_ref, conv_w_ref, conv_b_ref, w_a_ref, b_a_ref,
                 w_x_ref, b_x_ref, lam_ref, w_ro_ref, gf_ref,
                 y_ref, conv_out_ref, h_out_ref,
                 xpad_scr, hs_scr, hcar_scr, *, nb, tt):
    j = pl.program_id(1)
    rows = nb * tt
    d = x_ref.shape[-1]
    w = lam_ref.shape[-1]
    bw = w // LRU_BLOCKS
    ng = tt // SUBLANES

    v = jnp.concatenate(
        [_dot(o_ref[:, hd].reshape(rows, KV_LORA).astype(BF16), w_uv_ref[hd]) for hd in range(N_HEADS)],
        axis=1)
    gated = (v * zg_ref[...].reshape(rows, d).astype(F32)).astype(BF16)
    out0 = _dot(gated, w_o_ref[...]).reshape(nb, tt, d)
    x1 = x_ref[...] + gate0_ref[...] * out0

    h1 = _rms(x1, g1_ref[...]) * (1.0 + scale_ref[...]) + shift_ref[...]
    hb = h1.reshape(rows, d).astype(BF16)
    xb = _dot(hb, w_in_ref[:, 0:w])
    z = _dot(hb, w_in_ref[:, w:2 * w])

    @pl.when(j == 0)
    def _():
        xpad_scr[:, 0:SUBLANES - (CONV_W - 1), :] = jnp.zeros((nb, SUBLANES - (CONV_W - 1), w), F32)
        xpad_scr[:, SUBLANES - (CONV_W - 1):SUBLANES, :] = conv0_ref[...]
        hcar_scr[...] = jnp.broadcast_to(h0_ref[...], hcar_scr.shape)

    xpad_scr[:, SUBLANES:SUBLANES + tt, :] = xb.reshape(nb, tt, w)
    xc = conv_b_ref[...] + xpad_scr[:, SUBLANES - 3:SUBLANES - 3 + tt, :] * conv_w_ref[0:1, :]
    for k in range(1, CONV_W):
        off = SUBLANES - (CONV_W - 1) + k
        xc = xc + xpad_scr[:, off:off + tt, :] * conv_w_ref[k:k + 1, :]
    conv_out_ref[...] = xpad_scr[:, SUBLANES + tt - (CONV_W - 1):SUBLANES + tt, :]
    xpad_scr[:, 0:SUBLANES, :] = xpad_scr[:, tt:tt + SUBLANES, :]

    xc2 = xc.reshape(rows, w)
    xcb = xc2.astype(BF16)
    ra = jnp.concatenate(
        [_dot(xcb[:, n * bw:(n + 1) * bw], w_a_ref[n]) for n in range(LRU_BLOCKS)], axis=1) + b_a_ref[...]
    rx = jnp.concatenate(
        [_dot(xcb[:, n * bw:(n + 1) * bw], w_x_ref[n]) for n in range(LRU_BLOCKS)], axis=1) + b_x_ref[...]
    r = _sigmoid(ra)
    i = _sigmoid(rx)
    lam = lam_ref[...]
    neg_lam = -lam
    softplus = jnp.maximum(neg_lam, 0.0) + jnp.log1p(jnp.exp(-jnp.abs(neg_lam)))
    log_a = (-LRU_C * r) * softplus
    a = jnp.exp(log_a)
    one_minus_a2 = -jnp.tanh(log_a) * (a * a + 1.0)
    bq = jnp.sqrt(one_minus_a2) * (i * xc2)

    a_g = a.reshape(nb * ng, SUBLANES, w)
    b_g = bq.reshape(nb * ng, SUBLANES, w)
    tpos = lax.broadcasted_iota(jnp.int32, a_g.shape, 1)
    for sh in (1, 2, 4):
        keep = tpos >= sh
        a_prev = jnp.where(keep, pltpu.roll(a_g, sh, 1), 1.0)
        b_prev = jnp.where(keep, pltpu.roll(b_g, sh, 1), 0.0)
        b_g = a_g * b_prev + b_g
        a_g = a_g * a_prev
    a_g = a_g.reshape(nb, ng, SUBLANES, w)
    b_g = b_g.reshape(nb, ng, SUBLANES, w)
    hprev = hcar_scr[...]
    for g in range(ng):
        hg = a_g[:, g] * hprev + b_g[:, g]
        hs_scr[:, g * SUBLANES:(g + 1) * SUBLANES, :] = hg
        hprev = jnp.broadcast_to(hg[:, SUBLANES - 1:SUBLANES, :], hg.shape)
    hcar_scr[...] = hprev
    h_out_ref[...] = hprev[:, 0:1, :]

    hs = hs_scr[...].reshape(rows, w)
    gated1 = (hs * (z * _sigmoid(z))).astype(BF16)
    out1 = _dot(gated1, w_ro_ref[...]).reshape(nb, tt, d)
    x2 = x1 + gate1_ref[...] * out1
    y_ref[...] = _rms(x2, gf_ref[...])


def _back(o_lat, zg, x, mod0, mod1, conv0, h0, w_uv_p, w_o, g1, r_w_in, conv_w, conv_b, w_a, b_a, w_x, b_x,
          lam, r_w_o, final_g, *, nb, tt):
    bseq, t, d = x.shape
    w = lam.shape[-1]
    grid = (bseq // nb, t // tt)
    full = lambda shape: pl.BlockSpec(shape, lambda i, j: (0,) * len(shape))
    in_specs = [
        pl.BlockSpec((nb, N_HEADS, tt, KV_LORA), lambda i, j: (i, 0, j, 0)),
        pl.BlockSpec((nb, tt, d), lambda i, j: (i, j, 0)),
        pl.BlockSpec((nb, tt, d), lambda i, j: (i, j, 0)),
        pl.BlockSpec((nb, 1, d), lambda i, j: (i, 0, 2)),
        pl.BlockSpec((nb, 1, d), lambda i, j: (i, 0, 0)),
        pl.BlockSpec((nb, 1, d), lambda i, j: (i, 0, 1)),
        pl.BlockSpec((nb, 1, d), lambda i, j: (i, 0, 2)),
        pl.BlockSpec((nb, CONV_W - 1, w), lambda i, j: (i, 0, 0)),
        pl.BlockSpec((nb, 1, w), lambda i, j: (i, 0, 0)),
        full(w_uv_p.shape), full(w_o.shape), full((1, d)), full(r_w_in.shape),
        full(conv_w.shape), full((1, w)), full(w_a.shape), full((1, w)), full(w_x.shape), full((1, w)),
        full((1, w)), full(r_w_o.shape), full((1, d)),
    ]
    out_shape = (
        jax.ShapeDtypeStruct((bseq, t, d), F32),
        jax.ShapeDtypeStruct((bseq, CONV_W - 1, w), F32),
        jax.ShapeDtypeStruct((bseq, 1, w), F32),
    )
    out_specs = (
        pl.BlockSpec((nb, tt, d), lambda i, j: (i, j, 0)),
        pl.BlockSpec((nb, CONV_W - 1, w), lambda i, j: (i, 0, 0)),
        pl.BlockSpec((nb, 1, w), lambda i, j: (i, 0, 0)),
    )
    return pl.pallas_call(
        functools.partial(_back_kernel, nb=nb, tt=tt),
        out_shape=out_shape,
        grid=grid,
        in_specs=in_specs,
        out_specs=out_specs,
        scratch_shapes=[
            pltpu.VMEM((nb, tt + SUBLANES, w), F32),
            pltpu.VMEM((nb, tt, w), F32),
            pltpu.VMEM((nb, SUBLANES, w), F32),
        ],
        compiler_params=pltpu.CompilerParams(
            dimension_semantics=("arbitrary", "arbitrary"), vmem_limit_bytes=VMEM_LIMIT_BYTES),
        name="back",
    )(o_lat, zg, x, mod0, mod1, mod1, mod1, conv0, h0, w_uv_p, w_o, g1, r_w_in, conv_w, conv_b, w_a, b_a,
      w_x, b_x, lam, r_w_o, final_g)


def _rope_tables(pos):
    half = QK_ROPE // 2
    inv = ROPE_THETA ** (-jnp.arange(half, dtype=F32) / half)
    ang = pos.astype(F32)[:, None] * inv[None, :]
    cos, sin = jnp.cos(ang), jnp.sin(ang)
    cos_t = jnp.tile(jnp.concatenate([cos, cos], axis=1), (1, N_HEADS))
    sin_t = jnp.tile(jnp.concatenate([-sin, sin], axis=1), (1, N_HEADS))
    return cos_t, sin_t


def _tiles(bseq, t):
    tt = min(ROW_TILE, t)
    nb = max(1, min(bseq, ROW_TILE // tt))
    assert t % tt == 0 and bseq % nb == 0 and tt % SUBLANES == 0
    return nb, tt


def kernel(x_prompt, x_sample, c_prompt, c_sample, cache_ckv, cache_kpe, page_table, state_conv, state_h,
           norm_g, w_ada, b_ada, final_g,
           a_w_in, a_g_q, a_w_qb, a_g_kv, a_w_uk, a_w_uv, a_w_o,
           r_w_in, r_conv_w, r_conv_b, r_w_a, r_b_a, r_w_x, r_b_x, r_lam, r_w_o):
    bp, tp, d = x_prompt.shape
    bs, ts, _ = x_sample.shape
    assert norm_g.shape[0] == 2 and a_w_in.shape[0] == 1 and r_w_in.shape[0] == 1
    w = r_lam.shape[-1]
    past_len = page_table.shape[1] * PAGE_SIZE

    w_in = a_w_in[0]
    c_q, c_kv, c_pe = Q_LORA, Q_LORA + KV_LORA, Q_LORA + KV_LORA + QK_ROPE
    w_in_p = jnp.concatenate(
        [w_in[:, :c_kv], w_in[:, c_pe:], w_in[:, c_kv:c_pe], w_in[:, c_kv:c_pe]], axis=1).astype(BF16)
    w_qb = a_w_qb[0].reshape(Q_LORA, N_HEADS, QK_NOPE + QK_ROPE)
    w_qb_p = jnp.concatenate(
        [w_qb[:, :, :QK_NOPE].reshape(Q_LORA, N_HEADS * QK_NOPE),
         w_qb[:, :, QK_NOPE:].reshape(Q_LORA, N_HEADS * QK_ROPE)], axis=1).astype(BF16)
    w_uk_p = a_w_uk[0].transpose(1, 2, 0).astype(BF16)
    w_uv_p = a_w_uv[0].transpose(1, 0, 2).astype(BF16)
    w_o = a_w_o[0].astype(BF16)
    r_in = r_w_in[0].astype(BF16)
    w_a = r_w_a[0].astype(BF16)
    w_x = r_w_x[0].astype(BF16)
    r_o = r_w_o[0].astype(BF16)
    row = lambda v: v.reshape(1, -1)

    mod = _modulation(jnp.concatenate([c_prompt, c_sample], axis=0), w_ada, b_ada)
    mod_p = [mod[i, :bp].reshape(bp, 1, 3 * d) for i in range(2)]
    mod_s = [mod[i, bp:].reshape(bs, 1, 3 * d) for i in range(2)]

    cos_p, sin_p = _rope_tables(jnp.arange(tp))
    cos_s, sin_s = _rope_tables(past_len + jnp.arange(ts))

    nb_p, tt_p = _tiles(bp, tp)
    nb_s, tt_s = _tiles(bs, ts)

    front = functools.partial(_front, g=row(norm_g[0]), w_in_p=w_in_p, g_q=row(a_g_q[0]), w_qb_p=w_qb_p,
                              g_kv=row(a_g_kv[0]), w_uk_p=w_uk_p)
    qlat_p, qpe_p, ckv_p, kpe_p, ckvb_p, kpeb_p, zg_p = front(
        x_prompt, mod_p[0], cos_p, sin_p, nb=nb_p, tt=tt_p, q_dtype=BF16)
    qlat_s, qpe_s, ckv_s, kpe_s, _, _, zg_s = front(
        x_sample, mod_s[0], cos_s, sin_s, nb=nb_s, tt=tt_s, q_dtype=F32)

    o_p = _attn_prompt(qlat_p, qpe_p, ckvb_p, kpeb_p, tq=min(ROW_TILE, tp))
    o_s = _attn_sample(page_table, qlat_s, qpe_s, ckv_s, kpe_s, cache_ckv[0], cache_kpe[0])

    back = functools.partial(
        _back, w_uv_p=w_uv_p, w_o=w_o, g1=row(norm_g[1]), r_w_in=r_in, conv_w=r_conv_w[0],
        conv_b=row(r_conv_b[0]), w_a=w_a, b_a=row(r_b_a[0]), w_x=w_x, b_x=row(r_b_x[0]), lam=row(r_lam[0]),
        r_w_o=r_o, final_g=row(final_g))
    y_p, conv_p, h_p = back(o_p, zg_p, x_prompt, mod_p[0], mod_p[1],
                            jnp.zeros((bp, CONV_W - 1, w), F32), jnp.zeros((bp, 1, w), F32),
                            nb=nb_p, tt=tt_p)
    y_s, conv_s, h_s = back(o_s, zg_s, x_sample, mod_s[0], mod_s[1],
                            state_conv[0], state_h[0].reshape(bs, 1, w), nb=nb_s, tt=tt_s)

    return (y_p, y_s, ckv_p[None], kpe_p[None], ckv_s[None], kpe_s[None],
            conv_p[None], conv_s[None], h_p.reshape(1, bp, w), h_s.reshape(1, bs, w))
```

```python
import functools
import math

import jax
import jax.numpy as jnp
from jax import lax
from jax.experimental import pallas as pl
from jax.experimental.pallas import tpu as pltpu

F32 = jnp.float32
BF16 = jnp.bfloat16

N_HEADS = 8
QK_NOPE = 128
QK_ROPE = 64
V_DIM = 128
Q_LORA = 512
KV_LORA = 256
QK_CAT = KV_LORA + QK_ROPE
ROPE_THETA = 10000.0
SM_SCALE = 1.0 / math.sqrt(QK_NOPE + QK_ROPE)
Q_SCALE = SM_SCALE * math.log2(math.e)
NEG_INF = -1e30
LRU_BLOCKS = 4
CONV_W = 4
LRU_C = 8.0
NORM_EPS = 1e-6
PAGE_SIZE = 128

VMEM_LIMIT_BYTES = 56 * 1024 * 1024
SUBLANES = 8
ROW_TILE = 256


def _dot(a, b):
    return jnp.dot(a, b, preferred_element_type=F32)


def _dot_nt(a, b):
    return lax.dot_general(a, b, (((1,), (1,)), ((), ())), preferred_element_type=F32)


def _rms(x, g):
    return x * lax.rsqrt(jnp.mean(x * x, axis=-1, keepdims=True) + NORM_EPS) * g


def _sigmoid(x):
    return 0.5 * jnp.tanh(0.5 * x) + 0.5


def _params(*sem):
    return pltpu.CompilerParams(dimension_semantics=sem, vmem_limit_bytes=VMEM_LIMIT_BYTES)


def _full(shape):
    return pl.BlockSpec(shape, lambda *_: (0,) * len(shape))


def _mod_kernel(c_ref, w_ref, b_ref, o_ref):
    o_ref[0] = _dot(c_ref[...].astype(BF16), w_ref[0].astype(BF16)) + b_ref[0]


def _modulation(c_all, w_ada, b_ada):
    depth, d, d3 = w_ada.shape
    n = c_all.shape[0]
    return pl.pallas_call(
        _mod_kernel,
        out_shape=jax.ShapeDtypeStruct((depth, n, d3), F32),
        grid=(depth, d3 // d),
        in_specs=[
            pl.BlockSpec((n, d), lambda i, j: (0, 0)),
            pl.BlockSpec((1, d, d), lambda i, j: (i, 0, j)),
            pl.BlockSpec((1, 1, d), lambda i, j: (i, 0, j)),
        ],
        out_specs=pl.BlockSpec((1, n, d), lambda i, j: (i, 0, j)),
        compiler_params=_params("arbitrary", "arbitrary"),
        name="mod",
    )(c_all, w_ada, b_ada.reshape(depth, 1, d3))


def _rope_rows(x, cos, sin_signed, nb, tt):
    width = x.shape[-1]
    lane = lax.broadcasted_iota(jnp.int32, x.shape, 1)
    first_half = (lane & (QK_ROPE - 1)) < (QK_ROPE // 2)
    partner = jnp.where(first_half,
                        pltpu.roll(x, width - QK_ROPE // 2, 1),
                        pltpu.roll(x, QK_ROPE // 2, 1))
    x3 = x.reshape(nb, tt, width)
    p3 = partner.reshape(nb, tt, width)
    return x3 * cos[None] + p3 * sin_signed[None]


def _front_compute(x_ref, shift_ref, scale_ref, cos_ref, sin_ref, g_ref, w_in_ref, gq_ref, w_qb_ref, gkv_ref,
                   nb, tt):
    rows = nb * tt
    d = x_ref.shape[-1]
    h = _rms(x_ref[...], g_ref[...]) * (1.0 + scale_ref[...]) + shift_ref[...]
    hb = h.reshape(rows, d).astype(BF16)
    c0, c1, c2, c3 = Q_LORA, Q_LORA + KV_LORA, Q_LORA + KV_LORA + d, Q_LORA + KV_LORA + d + 2 * QK_ROPE
    u_q = _dot(hb, w_in_ref[:, 0:c0])
    u_kv = _dot(hb, w_in_ref[:, c0:c1])
    z = _dot(hb, w_in_ref[:, c1:c2])
    u_kpe = _dot(hb, w_in_ref[:, c2:c3])
    zg = z * _sigmoid(z)
    ckv = _rms(u_kv, gkv_ref[...])
    cos = cos_ref[...]
    sin = sin_ref[...]
    kr = _rope_rows(u_kpe, cos[:, 0:2 * QK_ROPE], sin[:, 0:2 * QK_ROPE], nb, tt)
    qn = _rms(u_q, gq_ref[...]).astype(BF16)
    q = _dot(qn, w_qb_ref[...])
    return zg, ckv, kr, q, cos, sin


def _front_prompt_kernel(x_ref, shift_ref, scale_ref, cos_ref, sin_ref, g_ref, w_in_ref, gq_ref, w_qb_ref,
                         gkv_ref, w_uk_ref,
                         qT_ref, ckv_ref, kpeT_ref, ckvb_ref, kpeb_ref, ckvT_ref, zg_ref, *, tt):
    d = x_ref.shape[-1]
    nope_w = N_HEADS * QK_NOPE
    pe_w = N_HEADS * QK_ROPE
    zg, ckv, kr, q, cos, sin = _front_compute(
        x_ref, shift_ref, scale_ref, cos_ref, sin_ref, g_ref, w_in_ref, gq_ref, w_qb_ref, gkv_ref, 1, tt)
    zg_ref[0] = zg.astype(zg_ref.dtype)
    ckv_ref[0] = ckv
    ckvb_ref[0] = ckv.astype(ckvb_ref.dtype)
    ckvT_ref[0, 0] = ckv.T.astype(ckvT_ref.dtype)
    kr2 = kr[0]
    kpeb_ref[0] = kr2[:, 0:QK_ROPE].astype(kpeb_ref.dtype)
    kpeT_ref[0] = kr2.T[0:QK_ROPE, :]
    for hd in range(N_HEADS):
        qh = q[:, hd * QK_NOPE:(hd + 1) * QK_NOPE].astype(BF16)
        ql = _dot(qh, w_uk_ref[hd]) * Q_SCALE
        qT_ref[0, 0, 0:KV_LORA, hd * tt:(hd + 1) * tt] = ql.T.astype(qT_ref.dtype)
    qrT = (_rope_rows(q[:, nope_w:nope_w + pe_w], cos, sin, 1, tt)[0] * Q_SCALE).T
    for hd in range(N_HEADS):
        qT_ref[0, 0, KV_LORA:QK_CAT, hd * tt:(hd + 1) * tt] = (
            qrT[hd * QK_ROPE:(hd + 1) * QK_ROPE, :].astype(qT_ref.dtype))


def _front_sample_kernel(x_ref, shift_ref, scale_ref, cos_ref, sin_ref, g_ref, w_in_ref, gq_ref, w_qb_ref,
                         gkv_ref, w_uk_ref,
                         qlat_ref, qpe_ref, ckv_ref, kpe_ref, zg_ref, *, nb, tt):
    d = x_ref.shape[-1]
    nope_w = N_HEADS * QK_NOPE
    pe_w = N_HEADS * QK_ROPE
    zg, ckv, kr, q, cos, sin = _front_compute(
        x_ref, shift_ref, scale_ref, cos_ref, sin_ref, g_ref, w_in_ref, gq_ref, w_qb_ref, gkv_ref, nb, tt)
    zg_ref[...] = zg.reshape(nb, tt, d).astype(zg_ref.dtype)
    ckv_ref[...] = ckv.reshape(nb, tt, KV_LORA)
    kpe_ref[...] = kr[:, :, 0:QK_ROPE]
    for hd in range(N_HEADS):
        qh = q[:, hd * QK_NOPE:(hd + 1) * QK_NOPE].astype(BF16)
        ql = _dot(qh, w_uk_ref[hd]) * Q_SCALE
        qlat_ref[:, hd] = ql.reshape(nb, tt, KV_LORA)
    qr = _rope_rows(q[:, nope_w:nope_w + pe_w], cos, sin, nb, tt) * Q_SCALE
    for hd in range(N_HEADS):
        qpe_ref[:, hd] = qr[:, :, hd * QK_ROPE:(hd + 1) * QK_ROPE]


def _front_in_specs(nb, tt, d, weights):
    pe_w = N_HEADS * QK_ROPE
    return [
        pl.BlockSpec((nb, tt, d), lambda i, j: (i, j, 0)),
        pl.BlockSpec((nb, 1, d), lambda i, j: (i, 0, 0)),
        pl.BlockSpec((nb, 1, d), lambda i, j: (i, 0, 1)),
        pl.BlockSpec((tt, pe_w), lambda i, j: (j, 0)),
        pl.BlockSpec((tt, pe_w), lambda i, j: (j, 0)),
    ] + [_full(w.shape) for w in weights]


def _front_prompt(x, mod3, cos_t, sin_t, weights, *, tt):
    b, t, d = x.shape
    nt = t // tt
    out_shape = (
        jax.ShapeDtypeStruct((b, nt, QK_CAT, N_HEADS * tt), BF16),
        jax.ShapeDtypeStruct((b, t, KV_LORA), F32),
        jax.ShapeDtypeStruct((b, QK_ROPE, t), F32),
        jax.ShapeDtypeStruct((b, t, KV_LORA), BF16),
        jax.ShapeDtypeStruct((b, t, QK_ROPE), BF16),
        jax.ShapeDtypeStruct((b, nt, KV_LORA, tt), BF16),
        jax.ShapeDtypeStruct((b, t, d), BF16),
    )
    out_specs = (
        pl.BlockSpec((1, 1, QK_CAT, N_HEADS * tt), lambda i, j: (i, j, 0, 0)),
        pl.BlockSpec((1, tt, KV_LORA), lambda i, j: (i, j, 0)),
        pl.BlockSpec((1, QK_ROPE, tt), lambda i, j: (i, 0, j)),
        pl.BlockSpec((1, tt, KV_LORA), lambda i, j: (i, j, 0)),
        pl.BlockSpec((1, tt, QK_ROPE), lambda i, j: (i, j, 0)),
        pl.BlockSpec((1, 1, KV_LORA, tt), lambda i, j: (i, j, 0, 0)),
        pl.BlockSpec((1, tt, d), lambda i, j: (i, j, 0)),
    )
    return pl.pallas_call(
        functools.partial(_front_prompt_kernel, tt=tt),
        out_shape=out_shape,
        grid=(b, nt),
        in_specs=_front_in_specs(1, tt, d, weights),
        out_specs=out_specs,
        compiler_params=_params("arbitrary", "arbitrary"),
        name="front_prompt",
    )(x, mod3, mod3, cos_t, sin_t, *weights)


def _front_sample(x, mod3, cos_t, sin_t, weights, *, nb, tt):
    bseq, t, d = x.shape
    out_shape = (
        jax.ShapeDtypeStruct((bseq, N_HEADS, t, KV_LORA), F32),
        jax.ShapeDtypeStruct((bseq, N_HEADS, t, QK_ROPE), F32),
        jax.ShapeDtypeStruct((bseq, t, KV_LORA), F32),
        jax.ShapeDtypeStruct((bseq, t, QK_ROPE), F32),
        jax.ShapeDtypeStruct((bseq, t, d), BF16),
    )
    out_specs = (
        pl.BlockSpec((nb, N_HEADS, tt, KV_LORA), lambda i, j: (i, 0, j, 0)),
        pl.BlockSpec((nb, N_HEADS, tt, QK_ROPE), lambda i, j: (i, 0, j, 0)),
        pl.BlockSpec((nb, tt, KV_LORA), lambda i, j: (i, j, 0)),
        pl.BlockSpec((nb, tt, QK_ROPE), lambda i, j: (i, j, 0)),
        pl.BlockSpec((nb, tt, d), lambda i, j: (i, j, 0)),
    )
    return pl.pallas_call(
        functools.partial(_front_sample_kernel, nb=nb, tt=tt),
        out_shape=out_shape,
        grid=(bseq // nb, t // tt),
        in_specs=_front_in_specs(nb, tt, d, weights),
        out_specs=out_specs,
        compiler_params=_params("arbitrary", "arbitrary"),
        name="front_sample",
    )(x, mod3, mod3, cos_t, sin_t, *weights)


def _attn_prompt_kernel(qT_ref, k_ref, kp_ref, vT_ref, w_uv_ref, o_ref, m_scr, l_scr, acc_scr, *, tq):
    qi = pl.program_id(1)
    cols = N_HEADS * tq
    qlT = qT_ref[0, 0, 0:KV_LORA, :]
    qpT = qT_ref[0, 0, KV_LORA:QK_CAT, :]

    m_scr[...] = jnp.full(m_scr.shape, NEG_INF, F32)
    l_scr[...] = jnp.zeros(l_scr.shape, F32)
    acc_scr[...] = jnp.zeros(acc_scr.shape, F32)

    def step(j, masked):
        start = pl.multiple_of(j * tq, tq)
        k = k_ref[0, pl.ds(start, tq), :]
        kp = kp_ref[0, pl.ds(start, tq), :]
        sT = _dot(k, qlT) + _dot(kp, qpT)
        if masked:
            kpos = lax.broadcasted_iota(jnp.int32, sT.shape, 0)
            qpos = lax.broadcasted_iota(jnp.int32, sT.shape, 1) & (tq - 1)
            sT = jnp.where(kpos <= qpos, sT, NEG_INF)
        m_prev = m_scr[...]
        m_new = jnp.maximum(m_prev, jnp.max(sT, axis=0, keepdims=True))
        alpha = jnp.exp2(m_prev - m_new)
        pT = jnp.exp2(sT - m_new)
        l_scr[...] = alpha * l_scr[...] + jnp.sum(pT, axis=0, keepdims=True)
        acc_scr[...] = alpha * acc_scr[...] + _dot(vT_ref[0, j], pT.astype(BF16))
        m_scr[...] = m_new

    def body(j, carry):
        step(j, False)
        return carry

    lax.fori_loop(0, qi, body, 0)
    step(qi, True)

    oT = acc_scr[...] / l_scr[...]
    for hd in range(N_HEADS):
        o_h = oT[:, hd * tq:(hd + 1) * tq].T.astype(BF16)
        o_ref[0, :, hd * V_DIM:(hd + 1) * V_DIM] = _dot(o_h, w_uv_ref[hd]).astype(o_ref.dtype)


def _attn_prompt(qT, ckvb, kpeb, ckvT, w_uv_p, *, tq):
    b, nt, _, cols = qT.shape
    t = nt * tq
    assert tq & (tq - 1) == 0
    return pl.pallas_call(
        functools.partial(_attn_prompt_kernel, tq=tq),
        out_shape=jax.ShapeDtypeStruct((b, t, N_HEADS * V_DIM), BF16),
        grid=(b, nt),
        in_specs=[
            pl.BlockSpec((1, 1, QK_CAT, cols), lambda i, j: (i, j, 0, 0)),
            pl.BlockSpec((1, t, KV_LORA), lambda i, j: (i, 0, 0)),
            pl.BlockSpec((1, t, QK_ROPE), lambda i, j: (i, 0, 0)),
            pl.BlockSpec((1, nt, KV_LORA, tq), lambda i, j: (i, 0, 0, 0)),
            _full(w_uv_p.shape),
        ],
        out_specs=pl.BlockSpec((1, tq, N_HEADS * V_DIM), lambda i, j: (i, j, 0)),
        scratch_shapes=[
            pltpu.VMEM((1, cols), F32),
            pltpu.VMEM((1, cols), F32),
            pltpu.VMEM((KV_LORA, cols), F32),
        ],
        compiler_params=_params("arbitrary", "arbitrary"),
        name="attn_prompt",
    )(qT, ckvb, kpeb, ckvT, w_uv_p)


SAMPLE_CHUNK_PAGES = 32
SAMPLE_SLOTS = 4


def _attn_sample_kernel(pt_ref, ql_ref, qp_ref, kn_ref, kpn_ref, cache_k, cache_pT, o_ref,
                        kbuf, pbuf, sems, *, n_chunks, cp, ts):
    b = pl.program_id(0)
    nseq = pl.num_programs(0)
    rows = N_HEADS * ts

    def page_copies(seq, chunk, slot, page):
        phys = pt_ref[seq, chunk * cp + page]
        dst = pl.ds(page * PAGE_SIZE, PAGE_SIZE)
        return (
            pltpu.make_async_copy(cache_k.at[phys], kbuf.at[slot, dst], sems.at[slot, 0]),
            pltpu.make_async_copy(cache_pT.at[phys], pbuf.at[slot, :, dst], sems.at[slot, 1]),
        )

    def start_chunk(seq, chunk, slot):
        for page in range(cp):
            for c in page_copies(seq, chunk, slot, page):
                c.start()

    def wait_chunk(seq, chunk, slot):
        for page in range(cp):
            for c in page_copies(seq, chunk, slot, page):
                c.wait()

    ahead = SAMPLE_SLOTS - 1

    @pl.when(b == 0)
    def _():
        for c in range(ahead):
            start_chunk(0, c, c % SAMPLE_SLOTS)

    ql = ql_ref[0].reshape(rows, KV_LORA).astype(BF16)
    qp = qp_ref[0].reshape(rows, QK_ROPE).astype(BF16)

    m = jnp.full((rows, 1), NEG_INF, F32)
    l = jnp.zeros((rows, 1), F32)
    acc = jnp.zeros((rows, KV_LORA), F32)

    def update(m, l, acc, s, v):
        m_new = jnp.maximum(m, jnp.max(s, axis=1, keepdims=True))
        alpha = jnp.exp2(m - m_new)
        p = jnp.exp2(s - m_new)
        l = alpha * l + jnp.sum(p, axis=1, keepdims=True)
        acc = alpha * acc + _dot(p.astype(BF16), v)
        return m_new, l, acc

    for c in range(n_chunks):
        slot = c % SAMPLE_SLOTS
        nxt = c + ahead
        nxt_slot = nxt % SAMPLE_SLOTS
        if nxt < n_chunks:
            start_chunk(b, nxt, nxt_slot)
        else:
            @pl.when(b + 1 < nseq)
            def _(nxt=nxt, nxt_slot=nxt_slot):
                start_chunk(b + 1, nxt - n_chunks, nxt_slot)
        wait_chunk(b, c, slot)
        k = kbuf[slot].astype(BF16)
        kpT = pbuf[slot].astype(BF16)
        s = _dot_nt(ql, k) + _dot(qp, kpT)
        m, l, acc = update(m, l, acc, s, k)

    kn = kn_ref[0].astype(BF16)
    kpn = kpn_ref[0].astype(BF16)
    s = _dot_nt(ql, kn) + _dot_nt(qp, kpn)
    s3 = s.reshape(N_HEADS, ts, ts)
    qpos = lax.broadcasted_iota(jnp.int32, s3.shape, 1)
    kpos = lax.broadcasted_iota(jnp.int32, s3.shape, 2)
    s = jnp.where(kpos <= qpos, s3, NEG_INF).reshape(rows, ts)
    m, l, acc = update(m, l, acc, s, kn)

    o_ref[0] = (acc / l).reshape(N_HEADS, ts, KV_LORA)


def _attn_sample(page_table, qlat, qpe, ckv_new, kpe_new, cache_ckv, cache_kpeT):
    bs, _, ts, _ = qlat.shape
    n_pages = page_table.shape[1]
    cp = min(SAMPLE_CHUNK_PAGES, n_pages // SAMPLE_SLOTS)
    assert n_pages % cp == 0
    n_chunks = n_pages // cp
    assert n_chunks % SAMPLE_SLOTS == 0
    grid_spec = pltpu.PrefetchScalarGridSpec(
        num_scalar_prefetch=1,
        grid=(bs,),
        in_specs=[
            pl.BlockSpec((1, N_HEADS, ts, KV_LORA), lambda i, pt: (i, 0, 0, 0)),
            pl.BlockSpec((1, N_HEADS, ts, QK_ROPE), lambda i, pt: (i, 0, 0, 0)),
            pl.BlockSpec((1, ts, KV_LORA), lambda i, pt: (i, 0, 0)),
            pl.BlockSpec((1, ts, QK_ROPE), lambda i, pt: (i, 0, 0)),
            pl.BlockSpec(memory_space=pl.ANY),
            pl.BlockSpec(memory_space=pl.ANY),
        ],
        out_specs=pl.BlockSpec((1, N_HEADS, ts, KV_LORA), lambda i, pt: (i, 0, 0, 0)),
        scratch_shapes=[
            pltpu.VMEM((SAMPLE_SLOTS, cp * PAGE_SIZE, KV_LORA), F32),
            pltpu.VMEM((SAMPLE_SLOTS, QK_ROPE, cp * PAGE_SIZE), F32),
            pltpu.SemaphoreType.DMA((SAMPLE_SLOTS, 2)),
        ],
    )
    return pl.pallas_call(
        functools.partial(_attn_sample_kernel, n_chunks=n_chunks, cp=cp, ts=ts),
        out_shape=jax.ShapeDtypeStruct((bs, N_HEADS, ts, KV_LORA), F32),
        grid_spec=grid_spec,
        compiler_params=_params("arbitrary"),
        name="attn_sample",
    )(page_table, qlat, qpe, ckv_new, kpe_new, cache_ckv, cache_kpeT)


def _back_kernel(o_ref, zg_ref, x_ref, gate0_ref, shift_ref, scale_ref, gate1_ref, conv0_ref, h0_ref,
                 w_uv_ref, w_o_ref, g1_ref, w_in_ref, conv_w_ref, conv_b_ref, w_a_ref, b_a_ref,
                 w_x_ref, b_x_ref, lam_ref, w_ro_ref, gf_ref,
                 y_ref, conv_out_ref, h_out_ref,
                 xpad_scr, hs_scr, hcar_scr, *, nb, tt, o_is_value):
    j = pl.program_id(1)
    rows = nb * tt
    d = x_ref.shape[-1]
    w = lam_ref.shape[-1]
    bw = w // LRU_BLOCKS
    ng = tt // SUBLANES

    if o_is_value:
        v = o_ref[...].reshape(rows, d).astype(F32)
    else:
        v = jnp.concatenate(
            [_dot(o_ref[:, hd].reshape(rows, KV_LORA).astype(BF16), w_uv_ref[hd]) for hd in range(N_HEADS)],
            axis=1)
    gated = (v * zg_ref[...].reshape(rows, d).astype(F32)).astype(BF16)
    out0 = _dot(gated, w_o_ref[...]).reshape(nb, tt, d)
    x1 = x_ref[...] + gate0_ref[...] * out0

    h1 = _rms(x1, g1_ref[...]) * (1.0 + scale_ref[...]) + shift_ref[...]
    hb = h1.reshape(rows, d).astype(BF16)
    xb = _dot(hb, w_in_ref[:, 0:w])
    z = _dot(hb, w_in_ref[:, w:2 * w])

    @pl.when(j == 0)
    def _():
        xpad_scr[:, 0:SUBLANES - (CONV_W - 1), :] = jnp.zeros((nb, SUBLANES - (CONV_W - 1), w), F32)
        xpad_scr[:, SUBLANES - (CONV_W - 1):SUBLANES, :] = conv0_ref[...]
        hcar_scr[...] = jnp.broadcast_to(h0_ref[...], hcar_scr.shape)

    xpad_scr[:, SUBLANES:SUBLANES + tt, :] = xb.reshape(nb, tt, w)
    xc = conv_b_ref[...] + xpad_scr[:, SUBLANES - 3:SUBLANES - 3 + tt, :] * conv_w_ref[0:1, :]
    for k in range(1, CONV_W):
        off = SUBLANES - (CONV_W - 1) + k
        xc = xc + xpad_scr[:, off:off + tt, :] * conv_w_ref[k:k + 1, :]
    conv_out_ref[...] = xpad_scr[:, SUBLANES + tt - (CONV_W - 1):SUBLANES + tt, :]
    xpad_scr[:, 0:SUBLANES, :] = xpad_scr[:, tt:tt + SUBLANES, :]

    xc2 = xc.reshape(rows, w)
    xcb = xc2.astype(BF16)
    ra = jnp.concatenate(
        [_dot(xcb[:, n * bw:(n + 1) * bw], w_a_ref[n]) for n in range(LRU_BLOCKS)], axis=1) + b_a_ref[...]
    rx = jnp.concatenate(
        [_dot(xcb[:, n * bw:(n + 1) * bw], w_x_ref[n]) for n in range(LRU_BLOCKS)], axis=1) + b_x_ref[...]
    r = _sigmoid(ra)
    i = _sigmoid(rx)
    neg_lam = -lam_ref[...]
    softplus = jnp.maximum(neg_lam, 0.0) + jnp.log1p(jnp.exp(-jnp.abs(neg_lam)))
    log_a = (-LRU_C * r) * softplus
    a = jnp.exp(log_a)
    one_minus_a2 = -jnp.tanh(log_a) * (a * a + 1.0)
    bq = jnp.sqrt(one_minus_a2) * (i * xc2)

    a_g = a.reshape(nb * ng, SUBLANES, w)
    b_g = bq.reshape(nb * ng, SUBLANES, w)
    tpos = lax.broadcasted_iota(jnp.int32, a_g.shape, 1)
    for sh in (1, 2, 4):
        keep = tpos >= sh
        a_prev = jnp.where(keep, pltpu.roll(a_g, sh, 1), 1.0)
        b_prev = jnp.where(keep, pltpu.roll(b_g, sh, 1), 0.0)
        b_g = a_g * b_prev + b_g
        a_g = a_g * a_prev
    a_g = a_g.reshape(nb, ng, SUBLANES, w)
    b_g = b_g.reshape(nb, ng, SUBLANES, w)
    hprev = hcar_scr[...]
    for g in range(ng):
        hg = a_g[:, g] * hprev + b_g[:, g]
        hs_scr[:, g * SUBLANES:(g + 1) * SUBLANES, :] = hg
        hprev = jnp.broadcast_to(hg[:, SUBLANES - 1:SUBLANES, :], hg.shape)
    hcar_scr[...] = hprev
    h_out_ref[...] = hprev[:, 0:1, :]

    hs = hs_scr[...].reshape(rows, w)
    gated1 = (hs * (z * _sigmoid(z))).astype(BF16)
    out1 = _dot(gated1, w_ro_ref[...]).reshape(nb, tt, d)
    x2 = x1 + gate1_ref[...] * out1
    y_ref[...] = _rms(x2, gf_ref[...])


def _back(o, zg, x, mod0, mod1, conv0, h0, weights, *, nb, tt, o_is_value):
    bseq, t, d = x.shape
    w = conv0.shape[-1]
    if o_is_value:
        o_spec = pl.BlockSpec((nb, tt, d), lambda i, j: (i, j, 0))
    else:
        o_spec = pl.BlockSpec((nb, N_HEADS, tt, KV_LORA), lambda i, j: (i, 0, j, 0))
    in_specs = [
        o_spec,
        pl.BlockSpec((nb, tt, d), lambda i, j: (i, j, 0)),
        pl.BlockSpec((nb, tt, d), lambda i, j: (i, j, 0)),
        pl.BlockSpec((nb, 1, d), lambda i, j: (i, 0, 2)),
        pl.BlockSpec((nb, 1, d), lambda i, j: (i, 0, 0)),
        pl.BlockSpec((nb, 1, d), lambda i, j: (i, 0, 1)),
        pl.BlockSpec((nb, 1, d), lambda i, j: (i, 0, 2)),
        pl.BlockSpec((nb, CONV_W - 1, w), lambda i, j: (i, 0, 0)),
        pl.BlockSpec((nb, 1, w), lambda i, j: (i, 0, 0)),
    ] + [_full(wt.shape) for wt in weights]
    out_shape = (
        jax.ShapeDtypeStruct((bseq, t, d), F32),
        jax.ShapeDtypeStruct((bseq, CONV_W - 1, w), F32),
        jax.ShapeDtypeStruct((bseq, 1, w), F32),
    )
    out_specs = (
        pl.BlockSpec((nb, tt, d), lambda i, j: (i, j, 0)),
        pl.BlockSpec((nb, CONV_W - 1, w), lambda i, j: (i, 0, 0)),
        pl.BlockSpec((nb, 1, w), lambda i, j: (i, 0, 0)),
    )
    return pl.pallas_call(
        functools.partial(_back_kernel, nb=nb, tt=tt, o_is_value=o_is_value),
        out_shape=out_shape,
        grid=(bseq // nb, t // tt),
        in_specs=in_specs,
        out_specs=out_specs,
        scratch_shapes=[
            pltpu.VMEM((nb, tt + SUBLANES, w), F32),
            pltpu.VMEM((nb, tt, w), F32),
            pltpu.VMEM((nb, SUBLANES, w), F32),
        ],
        compiler_params=_params("arbitrary", "arbitrary"),
        name="back_prompt" if o_is_value else "back_sample",
    )(o, zg, x, mod0, mod1, mod1, mod1, conv0, h0, *weights)


def _rope_tables(pos):
    half = QK_ROPE // 2
    inv = ROPE_THETA ** (-jnp.arange(half, dtype=F32) / half)
    ang = pos.astype(F32)[:, None] * inv[None, :]
    cos, sin = jnp.cos(ang), jnp.sin(ang)
    cos_t = jnp.tile(jnp.concatenate([cos, cos], axis=1), (1, N_HEADS))
    sin_t = jnp.tile(jnp.concatenate([-sin, sin], axis=1), (1, N_HEADS))
    return cos_t, sin_t


def _tiles(bseq, t):
    tt = min(ROW_TILE, t)
    nb = max(1, min(bseq, ROW_TILE // tt))
    assert t % tt == 0 and bseq % nb == 0 and tt % SUBLANES == 0
    return nb, tt


def kernel(x_prompt, x_sample, c_prompt, c_sample, cache_ckv, cache_kpe, page_table, state_conv, state_h,
           norm_g, w_ada, b_ada, final_g,
           a_w_in, a_g_q, a_w_qb, a_g_kv, a_w_uk, a_w_uv, a_w_o,
           r_w_in, r_conv_w, r_conv_b, r_w_a, r_b_a, r_w_x, r_b_x, r_lam, r_w_o):
    bp, tp, d = x_prompt.shape
    bs, ts, _ = x_sample.shape
    assert norm_g.shape[0] == 2 and a_w_in.shape[0] == 1 and r_w_in.shape[0] == 1
    w = r_lam.shape[-1]
    past_len = page_table.shape[1] * PAGE_SIZE

    w_in = a_w_in[0]
    c_kv, c_pe = Q_LORA + KV_LORA, Q_LORA + KV_LORA + QK_ROPE
    w_in_p = jnp.concatenate(
        [w_in[:, :c_kv], w_in[:, c_pe:], w_in[:, c_kv:c_pe], w_in[:, c_kv:c_pe]], axis=1).astype(BF16)
    w_qb = a_w_qb[0].reshape(Q_LORA, N_HEADS, QK_NOPE + QK_ROPE)
    w_qb_p = jnp.concatenate(
        [w_qb[:, :, :QK_NOPE].reshape(Q_LORA, N_HEADS * QK_NOPE),
         w_qb[:, :, QK_NOPE:].reshape(Q_LORA, N_HEADS * QK_ROPE)], axis=1).astype(BF16)
    w_uk_p = a_w_uk[0].transpose(1, 2, 0).astype(BF16)
    w_uv_p = a_w_uv[0].transpose(1, 0, 2).astype(BF16)
    row = lambda v: v.reshape(1, -1)
    front_w = (row(norm_g[0]), w_in_p, row(a_g_q[0]), w_qb_p, row(a_g_kv[0]), w_uk_p)
    back_w = (w_uv_p, a_w_o[0].astype(BF16), row(norm_g[1]), r_w_in[0].astype(BF16), r_conv_w[0],
              row(r_conv_b[0]), r_w_a[0].astype(BF16), row(r_b_a[0]), r_w_x[0].astype(BF16), row(r_b_x[0]),
              row(r_lam[0]), r_w_o[0].astype(BF16), row(final_g))

    mod = _modulation(jnp.concatenate([c_prompt, c_sample], axis=0), w_ada, b_ada)
    mod_p = [mod[i, :bp].reshape(bp, 1, 3 * d) for i in range(2)]
    mod_s = [mod[i, bp:].reshape(bs, 1, 3 * d) for i in range(2)]

    cos_p, sin_p = _rope_tables(jnp.arange(tp))
    cos_s, sin_s = _rope_tables(past_len + jnp.arange(ts))

    _, tt_p = _tiles(bp, tp)
    nb_s, tt_s = _tiles(bs, ts)

    qT_p, ckv_p, kpeT_p, ckvb_p, kpeb_p, ckvT_p, zg_p = _front_prompt(
        x_prompt, mod_p[0], cos_p, sin_p, front_w, tt=tt_p)
    qlat_s, qpe_s, ckv_s, kpe_s, zg_s = _front_sample(
        x_sample, mod_s[0], cos_s, sin_s, front_w, nb=nb_s, tt=tt_s)

    v_p = _attn_prompt(qT_p, ckvb_p, kpeb_p, ckvT_p, w_uv_p, tq=tt_p)
    o_s = _attn_sample(page_table, qlat_s, qpe_s, ckv_s, kpe_s, cache_ckv[0], jnp.swapaxes(cache_kpe[0], 1, 2))

    y_p, conv_p, h_p = _back(v_p, zg_p, x_prompt, mod_p[0], mod_p[1],
                             jnp.zeros((bp, CONV_W - 1, w), F32), jnp.zeros((bp, 1, w), F32),
                             back_w, nb=1, tt=tt_p, o_is_value=True)
    y_s, conv_s, h_s = _back(o_s, zg_s, x_sample, mod_s[0], mod_s[1],
                             state_conv[0], state_h[0].reshape(bs, 1, w),
                             back_w, nb=nb_s, tt=tt_s, o_is_value=False)

    return (y_p, y_s, ckv_p[None], jnp.swapaxes(kpeT_p, 1, 2)[None], ckv_s[None], kpe_s[None],
            conv_p[None], conv_s[None], h_p.reshape(1, bp, w), h_s.reshape(1, bs, w))
```

```python
import functools
import math

import jax
import jax.numpy as jnp
from jax import lax
from jax.experimental import pallas as pl
from jax.experimental.pallas import tpu as pltpu

F32 = jnp.float32
BF16 = jnp.bfloat16

N_HEADS = 8
QK_NOPE = 128
QK_ROPE = 64
V_DIM = 128
Q_LORA = 512
KV_LORA = 256
QK_CAT = KV_LORA + QK_ROPE
ROPE_THETA = 10000.0
SM_SCALE = 1.0 / math.sqrt(QK_NOPE + QK_ROPE)
Q_SCALE = SM_SCALE * math.log2(math.e)
NEG_INF = -1e30
LRU_BLOCKS = 4
CONV_W = 4
LRU_C = 8.0
NORM_EPS = 1e-6
PAGE_SIZE = 128

VMEM_LIMIT_BYTES = 56 * 1024 * 1024
SUBLANES = 8
ROW_TILE = 256


def _dot(a, b):
    return jnp.dot(a, b, preferred_element_type=F32)


def _dot_nt(a, b):
    return lax.dot_general(a, b, (((1,), (1,)), ((), ())), preferred_element_type=F32)


def _rms(x, g):
    return x * lax.rsqrt(jnp.mean(x * x, axis=-1, keepdims=True) + NORM_EPS) * g


def _silu(x):
    half = 0.5 * x
    return half * jnp.tanh(half) + half


def _params(*sem):
    return pltpu.CompilerParams(dimension_semantics=sem, vmem_limit_bytes=VMEM_LIMIT_BYTES)


def _full(shape):
    return pl.BlockSpec(shape, lambda *_: (0,) * len(shape))


def _mod_kernel(c_ref, w_ref, b_ref, o_ref):
    o_ref[0] = _dot(c_ref[...].astype(BF16), w_ref[0].astype(BF16)) + b_ref[0]


def _modulation(c_all, w_ada, b_ada):
    depth, d, d3 = w_ada.shape
    n = c_all.shape[0]
    return pl.pallas_call(
        _mod_kernel,
        out_shape=jax.ShapeDtypeStruct((depth, n, d3), F32),
        grid=(depth, d3 // d),
        in_specs=[
            pl.BlockSpec((n, d), lambda i, j: (0, 0)),
            pl.BlockSpec((1, d, d), lambda i, j: (i, 0, j)),
            pl.BlockSpec((1, 1, d), lambda i, j: (i, 0, j)),
        ],
        out_specs=pl.BlockSpec((1, n, d), lambda i, j: (i, 0, j)),
        compiler_params=_params("arbitrary", "arbitrary"),
        name="mod",
    )(c_all, w_ada, b_ada.reshape(depth, 1, d3))


def _rope_rows(x, cos, sin_signed, nb, tt):
    width = x.shape[-1]
    lane = lax.broadcasted_iota(jnp.int32, x.shape, 1)
    first_half = (lane & (QK_ROPE - 1)) < (QK_ROPE // 2)
    partner = jnp.where(first_half,
                        pltpu.roll(x, width - QK_ROPE // 2, 1),
                        pltpu.roll(x, QK_ROPE // 2, 1))
    x3 = x.reshape(nb, tt, width)
    p3 = partner.reshape(nb, tt, width)
    return x3 * cos[None] + p3 * sin_signed[None]


def _front_compute(x_ref, shift_ref, scale_ref, cos_ref, sin_ref, g_ref, w_in_ref, gq_ref, w_qb_ref, gkv_ref,
                   nb, tt):
    rows = nb * tt
    d = x_ref.shape[-1]
    h = _rms(x_ref[...], g_ref[...]) * (1.0 + scale_ref[...]) + shift_ref[...]
    hb = h.reshape(rows, d).astype(BF16)
    c0, c1, c2, c3 = Q_LORA, Q_LORA + KV_LORA, Q_LORA + KV_LORA + d, Q_LORA + KV_LORA + d + 2 * QK_ROPE
    u_q = _dot(hb, w_in_ref[:, 0:c0])
    u_kv = _dot(hb, w_in_ref[:, c0:c1])
    z = _dot(hb, w_in_ref[:, c1:c2])
    u_kpe = _dot(hb, w_in_ref[:, c2:c3])
    zg = _silu(z)
    ckv = _rms(u_kv, gkv_ref[...])
    cos = cos_ref[...]
    sin = sin_ref[...]
    kr = _rope_rows(u_kpe, cos[:, 0:2 * QK_ROPE], sin[:, 0:2 * QK_ROPE], nb, tt)
    qn = _rms(u_q, gq_ref[...]).astype(BF16)
    q = _dot(qn, w_qb_ref[...])
    return zg, ckv, kr, q, cos, sin


def _front_prompt_kernel(x_ref, shift_ref, scale_ref, cos_ref, sin_ref, g_ref, w_in_ref, gq_ref, w_qb_ref,
                         gkv_ref, w_uk_ref,
                         qT_ref, ckv_ref, kpeT_ref, ckvb_ref, kpeb_ref, ckvT_ref, zg_ref, *, tt):
    d = x_ref.shape[-1]
    nope_w = N_HEADS * QK_NOPE
    pe_w = N_HEADS * QK_ROPE
    zg, ckv, kr, q, cos, sin = _front_compute(
        x_ref, shift_ref, scale_ref, cos_ref, sin_ref, g_ref, w_in_ref, gq_ref, w_qb_ref, gkv_ref, 1, tt)
    zg_ref[0] = zg.astype(zg_ref.dtype)
    ckv_ref[0] = ckv
    ckvb_ref[0] = ckv.astype(ckvb_ref.dtype)
    ckvT_ref[0, 0] = ckv.T.astype(ckvT_ref.dtype)
    kr2 = kr[0]
    kpeb_ref[0] = kr2[:, 0:QK_ROPE].astype(kpeb_ref.dtype)
    kpeT_ref[0] = kr2.T[0:QK_ROPE, :]
    for hd in range(N_HEADS):
        qh = q[:, hd * QK_NOPE:(hd + 1) * QK_NOPE].astype(BF16)
        ql = _dot(qh, w_uk_ref[hd]) * Q_SCALE
        qT_ref[0, 0, 0:KV_LORA, hd * tt:(hd + 1) * tt] = ql.T.astype(qT_ref.dtype)
    qrT = (_rope_rows(q[:, nope_w:nope_w + pe_w], cos, sin, 1, tt)[0] * Q_SCALE).T
    for hd in range(N_HEADS):
        qT_ref[0, 0, KV_LORA:QK_CAT, hd * tt:(hd + 1) * tt] = (
            qrT[hd * QK_ROPE:(hd + 1) * QK_ROPE, :].astype(qT_ref.dtype))


def _front_sample_kernel(x_ref, shift_ref, scale_ref, cos_ref, sin_ref, g_ref, w_in_ref, gq_ref, w_qb_ref,
                         gkv_ref, w_uk_ref,
                         qlat_ref, qpe_ref, ckv_ref, kpe_ref, zg_ref, *, nb, tt):
    d = x_ref.shape[-1]
    nope_w = N_HEADS * QK_NOPE
    pe_w = N_HEADS * QK_ROPE
    zg, ckv, kr, q, cos, sin = _front_compute(
        x_ref, shift_ref, scale_ref, cos_ref, sin_ref, g_ref, w_in_ref, gq_ref, w_qb_ref, gkv_ref, nb, tt)
    zg_ref[...] = zg.reshape(nb, tt, d).astype(zg_ref.dtype)
    ckv_ref[...] = ckv.reshape(nb, tt, KV_LORA)
    kpe_ref[...] = kr[:, :, 0:QK_ROPE]
    for hd in range(N_HEADS):
        qh = q[:, hd * QK_NOPE:(hd + 1) * QK_NOPE].astype(BF16)
        ql = _dot(qh, w_uk_ref[hd]) * Q_SCALE
        qlat_ref[:, hd] = ql.reshape(nb, tt, KV_LORA)
    qr = _rope_rows(q[:, nope_w:nope_w + pe_w], cos, sin, nb, tt) * Q_SCALE
    for hd in range(N_HEADS):
        qpe_ref[:, hd] = qr[:, :, hd * QK_ROPE:(hd + 1) * QK_ROPE]


def _front_in_specs(nb, tt, d, weights):
    pe_w = N_HEADS * QK_ROPE
    return [
        pl.BlockSpec((nb, tt, d), lambda i, j: (i, j, 0)),
        pl.BlockSpec((nb, 1, d), lambda i, j: (i, 0, 0)),
        pl.BlockSpec((nb, 1, d), lambda i, j: (i, 0, 1)),
        pl.BlockSpec((tt, pe_w), lambda i, j: (j, 0)),
        pl.BlockSpec((tt, pe_w), lambda i, j: (j, 0)),
    ] + [_full(w.shape) for w in weights]


def _front_prompt(x, mod3, cos_t, sin_t, weights, *, tt):
    b, t, d = x.shape
    nt = t // tt
    out_shape = (
        jax.ShapeDtypeStruct((b, nt, QK_CAT, N_HEADS * tt), BF16),
        jax.ShapeDtypeStruct((b, t, KV_LORA), F32),
        jax.ShapeDtypeStruct((b, QK_ROPE, t), F32),
        jax.ShapeDtypeStruct((b, t, KV_LORA), BF16),
        jax.ShapeDtypeStruct((b, t, QK_ROPE), BF16),
        jax.ShapeDtypeStruct((b, nt, KV_LORA, tt), BF16),
        jax.ShapeDtypeStruct((b, t, d), BF16),
    )
    out_specs = (
        pl.BlockSpec((1, 1, QK_CAT, N_HEADS * tt), lambda i, j: (i, j, 0, 0)),
        pl.BlockSpec((1, tt, KV_LORA), lambda i, j: (i, j, 0)),
        pl.BlockSpec((1, QK_ROPE, tt), lambda i, j: (i, 0, j)),
        pl.BlockSpec((1, tt, KV_LORA), lambda i, j: (i, j, 0)),
        pl.BlockSpec((1, tt, QK_ROPE), lambda i, j: (i, j, 0)),
        pl.BlockSpec((1, 1, KV_LORA, tt), lambda i, j: (i, j, 0, 0)),
        pl.BlockSpec((1, tt, d), lambda i, j: (i, j, 0)),
    )
    return pl.pallas_call(
        functools.partial(_front_prompt_kernel, tt=tt),
        out_shape=out_shape,
        grid=(b, nt),
        in_specs=_front_in_specs(1, tt, d, weights),
        out_specs=out_specs,
        compiler_params=_params("arbitrary", "arbitrary"),
        name="front_prompt",
    )(x, mod3, mod3, cos_t, sin_t, *weights)


def _front_sample(x, mod3, cos_t, sin_t, weights, *, nb, tt):
    bseq, t, d = x.shape
    out_shape = (
        jax.ShapeDtypeStruct((bseq, N_HEADS, t, KV_LORA), F32),
        jax.ShapeDtypeStruct((bseq, N_HEADS, t, QK_ROPE), F32),
        jax.ShapeDtypeStruct((bseq, t, KV_LORA), F32),
        jax.ShapeDtypeStruct((bseq, t, QK_ROPE), F32),
        jax.ShapeDtypeStruct((bseq, t, d), BF16),
    )
    out_specs = (
        pl.BlockSpec((nb, N_HEADS, tt, KV_LORA), lambda i, j: (i, 0, j, 0)),
        pl.BlockSpec((nb, N_HEADS, tt, QK_ROPE), lambda i, j: (i, 0, j, 0)),
        pl.BlockSpec((nb, tt, KV_LORA), lambda i, j: (i, j, 0)),
        pl.BlockSpec((nb, tt, QK_ROPE), lambda i, j: (i, j, 0)),
        pl.BlockSpec((nb, tt, d), lambda i, j: (i, j, 0)),
    )
    return pl.pallas_call(
        functools.partial(_front_sample_kernel, nb=nb, tt=tt),
        out_shape=out_shape,
        grid=(bseq // nb, t // tt),
        in_specs=_front_in_specs(nb, tt, d, weights),
        out_specs=out_specs,
        compiler_params=_params("arbitrary", "arbitrary"),
        name="front_sample",
    )(x, mod3, mod3, cos_t, sin_t, *weights)


def _attn_prompt_kernel(qT_ref, k_ref, kp_ref, vT_ref, w_uv_ref, o_ref, m_scr, l_scr, acc_scr, sa_scr, sb_scr,
                        *, tq):
    qi = pl.program_id(1)
    cols = N_HEADS * tq
    qlT = qT_ref[0, 0, 0:KV_LORA, :]
    qpT = qT_ref[0, 0, KV_LORA:QK_CAT, :]

    m_scr[...] = jnp.full(m_scr.shape, NEG_INF, F32)
    l_scr[...] = jnp.zeros(l_scr.shape, F32)
    acc_scr[...] = jnp.zeros(acc_scr.shape, F32)

    def scores(j):
        start = pl.multiple_of(j * tq, tq)
        k = k_ref[0, pl.ds(start, tq), :]
        kp = kp_ref[0, pl.ds(start, tq), :]
        return _dot(k, qlT) + _dot(kp, qpT)

    def accumulate(j, sT, masked):
        if masked:
            kpos = lax.broadcasted_iota(jnp.int32, sT.shape, 0)
            qpos = lax.broadcasted_iota(jnp.int32, sT.shape, 1) & (tq - 1)
            sT = jnp.where(kpos <= qpos, sT, NEG_INF)
        m_prev = m_scr[...]
        m_new = jnp.maximum(m_prev, jnp.max(sT, axis=0, keepdims=True))
        alpha = jnp.exp2(m_prev - m_new)
        pT = jnp.exp2(sT - m_new)
        l_scr[...] = alpha * l_scr[...] + jnp.sum(pT, axis=0, keepdims=True)
        acc_scr[...] = alpha * acc_scr[...] + _dot(vT_ref[0, j], pT.astype(BF16))
        m_scr[...] = m_new

    sa_scr[...] = scores(0)

    def body(i, carry):
        j = 2 * i
        sb_scr[...] = scores(j + 1)
        accumulate(j, sa_scr[...], False)
        sa_scr[...] = scores(j + 2)
        accumulate(j + 1, sb_scr[...], False)
        return carry

    lax.fori_loop(0, qi // 2, body, 0)

    @pl.when(qi % 2 == 0)
    def _():
        accumulate(qi, sa_scr[...], True)

    @pl.when(qi % 2 == 1)
    def _():
        sb_scr[...] = scores(qi)
        accumulate(qi - 1, sa_scr[...], False)
        accumulate(qi, sb_scr[...], True)

    oT = acc_scr[...] / l_scr[...]
    for hd in range(N_HEADS):
        o_h = oT[:, hd * tq:(hd + 1) * tq].T.astype(BF16)
        o_ref[0, :, hd * V_DIM:(hd + 1) * V_DIM] = _dot(o_h, w_uv_ref[hd]).astype(o_ref.dtype)


def _attn_prompt(qT, ckvb, kpeb, ckvT, w_uv_p, *, tq):
    b, nt, _, cols = qT.shape
    t = nt * tq
    assert tq & (tq - 1) == 0
    return pl.pallas_call(
        functools.partial(_attn_prompt_kernel, tq=tq),
        out_shape=jax.ShapeDtypeStruct((b, t, N_HEADS * V_DIM), BF16),
        grid=(b, nt),
        in_specs=[
            pl.BlockSpec((1, 1, QK_CAT, cols), lambda i, j: (i, j, 0, 0)),
            pl.BlockSpec((1, t, KV_LORA), lambda i, j: (i, 0, 0)),
            pl.BlockSpec((1, t, QK_ROPE), lambda i, j: (i, 0, 0)),
            pl.BlockSpec((1, nt, KV_LORA, tq), lambda i, j: (i, 0, 0, 0)),
            _full(w_uv_p.shape),
        ],
        out_specs=pl.BlockSpec((1, tq, N_HEADS * V_DIM), lambda i, j: (i, j, 0)),
        scratch_shapes=[
            pltpu.VMEM((1, cols), F32),
            pltpu.VMEM((1, cols), F32),
            pltpu.VMEM((KV_LORA, cols), F32),
            pltpu.VMEM((tq, cols), F32),
            pltpu.VMEM((tq, cols), F32),
        ],
        compiler_params=_params("arbitrary", "arbitrary"),
        name="attn_prompt",
    )(qT, ckvb, kpeb, ckvT, w_uv_p)


SAMPLE_CHUNK_PAGES = 32
SAMPLE_SLOTS = 4


def _attn_sample_kernel(pt_ref, ql_ref, qp_ref, kn_ref, kpn_ref, cache_k, cache_pT, o_ref,
                        kbuf, pbuf, kb_scr, sems, *, n_chunks, cp, ts):
    b = pl.program_id(0)
    nseq = pl.num_programs(0)
    rows = N_HEADS * ts

    def page_copies(seq, chunk, slot, page):
        phys = pt_ref[seq, chunk * cp + page]
        dst = pl.ds(page * PAGE_SIZE, PAGE_SIZE)
        return (
            pltpu.make_async_copy(cache_k.at[phys], kbuf.at[slot, dst], sems.at[slot, 0]),
            pltpu.make_async_copy(cache_pT.at[phys], pbuf.at[slot, :, dst], sems.at[slot, 1]),
        )

    def start_chunk(seq, chunk, slot):
        for page in range(cp):
            for c in page_copies(seq, chunk, slot, page):
                c.start()

    def wait_chunk(seq, chunk, slot):
        for page in range(cp):
            for c in page_copies(seq, chunk, slot, page):
                c.wait()

    ahead = SAMPLE_SLOTS - 1

    @pl.when(b == 0)
    def _():
        for c in range(ahead):
            start_chunk(0, c, c % SAMPLE_SLOTS)

    ql = ql_ref[0].reshape(rows, KV_LORA).astype(BF16)
    qp = qp_ref[0].reshape(rows, QK_ROPE).astype(BF16)

    m = jnp.full((rows, 1), NEG_INF, F32)
    l = jnp.zeros((rows, 1), F32)
    acc = jnp.zeros((rows, KV_LORA), F32)

    def update(m, l, acc, s, v):
        m_new = jnp.maximum(m, jnp.max(s, axis=1, keepdims=True))
        alpha = jnp.exp2(m - m_new)
        p = jnp.exp2(s - m_new)
        l = alpha * l + jnp.sum(p, axis=1, keepdims=True)
        acc = alpha * acc + _dot(p.astype(BF16), v)
        return m_new, l, acc

    s_prev = None
    for c in range(n_chunks + 1):
        if c < n_chunks:
            slot = c % SAMPLE_SLOTS
            nxt = c + ahead
            nxt_slot = nxt % SAMPLE_SLOTS
            if nxt < n_chunks:
                start_chunk(b, nxt, nxt_slot)
            else:
                @pl.when(b + 1 < nseq)
                def _(nxt=nxt, nxt_slot=nxt_slot):
                    start_chunk(b + 1, nxt - n_chunks, nxt_slot)
            wait_chunk(b, c, slot)
            k = kbuf[slot].astype(BF16)
            kb_scr[c % 2] = k
            s_cur = _dot_nt(ql, k) + _dot(qp, pbuf[slot].astype(BF16))
        if c == 0:
            kn = kn_ref[0].astype(BF16)
            kpn = kpn_ref[0].astype(BF16)
            s = _dot_nt(ql, kn) + _dot_nt(qp, kpn)
            s3 = s.reshape(N_HEADS, ts, ts)
            qpos = lax.broadcasted_iota(jnp.int32, s3.shape, 1)
            kpos = lax.broadcasted_iota(jnp.int32, s3.shape, 2)
            s = jnp.where(kpos <= qpos, s3, NEG_INF).reshape(rows, ts)
            m, l, acc = update(m, l, acc, s, kn)
        else:
            m, l, acc = update(m, l, acc, s_prev, kb_scr[(c - 1) % 2])
        s_prev = s_cur

    o_ref[0] = (acc / l).reshape(N_HEADS, ts, KV_LORA)


def _attn_sample(page_table, qlat, qpe, ckv_new, kpe_new, cache_ckv, cache_kpeT):
    bs, _, ts, _ = qlat.shape
    n_pages = page_table.shape[1]
    cp = min(SAMPLE_CHUNK_PAGES, n_pages // SAMPLE_SLOTS)
    assert n_pages % cp == 0
    n_chunks = n_pages // cp
    assert n_chunks % SAMPLE_SLOTS == 0
    grid_spec = pltpu.PrefetchScalarGridSpec(
        num_scalar_prefetch=1,
        grid=(bs,),
        in_specs=[
            pl.BlockSpec((1, N_HEADS, ts, KV_LORA), lambda i, pt: (i, 0, 0, 0)),
            pl.BlockSpec((1, N_HEADS, ts, QK_ROPE), lambda i, pt: (i, 0, 0, 0)),
            pl.BlockSpec((1, ts, KV_LORA), lambda i, pt: (i, 0, 0)),
            pl.BlockSpec((1, ts, QK_ROPE), lambda i, pt: (i, 0, 0)),
            pl.BlockSpec(memory_space=pl.ANY),
            pl.BlockSpec(memory_space=pl.ANY),
        ],
        out_specs=pl.BlockSpec((1, N_HEADS, ts, KV_LORA), lambda i, pt: (i, 0, 0, 0)),
        scratch_shapes=[
            pltpu.VMEM((SAMPLE_SLOTS, cp * PAGE_SIZE, KV_LORA), F32),
            pltpu.VMEM((SAMPLE_SLOTS, QK_ROPE, cp * PAGE_SIZE), F32),
            pltpu.VMEM((2, cp * PAGE_SIZE, KV_LORA), BF16),
            pltpu.SemaphoreType.DMA((SAMPLE_SLOTS, 2)),
        ],
    )
    return pl.pallas_call(
        functools.partial(_attn_sample_kernel, n_chunks=n_chunks, cp=cp, ts=ts),
        out_shape=jax.ShapeDtypeStruct((bs, N_HEADS, ts, KV_LORA), F32),
        grid_spec=grid_spec,
        compiler_params=_params("arbitrary"),
        name="attn_sample",
    )(page_table, qlat, qpe, ckv_new, kpe_new, cache_ckv, cache_kpeT)


def _back_kernel(o_ref, zg_ref, x_ref, gate0_ref, shift_ref, scale_ref, gate1_ref, conv0_ref, h0_ref,
                 w_uv_ref, w_o_ref, g1_ref, w_in_ref, conv_w_ref, conv_b_ref, w_a_ref, b_a_ref,
                 w_x_ref, b_x_ref, lam_ref, w_ro_ref, gf_ref,
                 y_ref, conv_out_ref, h_out_ref,
                 xprev_scr, hs_scr, hcar_scr, *, nb, tt, o_is_value):
    j = pl.program_id(1)
    rows = nb * tt
    d = x_ref.shape[-1]
    w = lam_ref.shape[-1]
    bw = w // LRU_BLOCKS
    ng = tt // SUBLANES

    if o_is_value:
        v = o_ref[...].reshape(rows, d).astype(F32)
    else:
        v = jnp.concatenate(
            [_dot(o_ref[:, hd].reshape(rows, KV_LORA).astype(BF16), w_uv_ref[hd]) for hd in range(N_HEADS)],
            axis=1)
    gated = (v * zg_ref[...].reshape(rows, d).astype(F32)).astype(BF16)
    out0 = _dot(gated, w_o_ref[...]).reshape(nb, tt, d)
    x1 = x_ref[...] + gate0_ref[...] * out0

    h1 = _rms(x1, g1_ref[...]) * (1.0 + scale_ref[...]) + shift_ref[...]
    hb = h1.reshape(rows, d).astype(BF16)
    xb = _dot(hb, w_in_ref[:, 0:w])
    z = _dot(hb, w_in_ref[:, w:2 * w])

    @pl.when(j == 0)
    def _():
        xprev_scr[:, 0:SUBLANES - (CONV_W - 1), :] = jnp.zeros((nb, SUBLANES - (CONV_W - 1), w), F32)
        xprev_scr[:, SUBLANES - (CONV_W - 1):SUBLANES, :] = conv0_ref[...]
        hcar_scr[...] = jnp.broadcast_to(h0_ref[...], hcar_scr.shape)

    xb3 = xb.reshape(nb, tt, w)
    ext = jnp.concatenate([xprev_scr[...], xb3], axis=1).reshape(nb * (ng + 1), SUBLANES, w)
    xprev_scr[...] = xb3[:, tt - SUBLANES:tt, :]
    conv_out_ref[...] = xb3[:, tt - (CONV_W - 1):tt, :]
    gshape = (nb, ng, SUBLANES, w)
    gpos = lax.broadcasted_iota(jnp.int32, gshape, 2)
    xc = conv_b_ref[...] + xb3.reshape(gshape) * conv_w_ref[CONV_W - 1:CONV_W, :]
    for sh in range(1, CONV_W):
        rolled = pltpu.roll(ext, sh, 1).reshape(nb, ng + 1, SUBLANES, w)
        shifted = jnp.where(gpos >= sh, rolled[:, 1:], rolled[:, :ng])
        xc = xc + shifted * conv_w_ref[CONV_W - 1 - sh:CONV_W - sh, :]

    xc2 = xc.reshape(rows, w)
    xcb = xc2.astype(BF16)
    t_r = jnp.tanh(jnp.concatenate(
        [_dot(xcb[:, n * bw:(n + 1) * bw], w_a_ref[n]) for n in range(LRU_BLOCKS)], axis=1) + b_a_ref[...])
    t_i = jnp.tanh(jnp.concatenate(
        [_dot(xcb[:, n * bw:(n + 1) * bw], w_x_ref[n]) for n in range(LRU_BLOCKS)], axis=1) + b_x_ref[...])
    neg_lam = -lam_ref[...]
    softplus = jnp.maximum(neg_lam, 0.0) + jnp.log1p(jnp.exp(-jnp.abs(neg_lam)))
    half_rate = (-0.5 * LRU_C) * softplus
    log_a = half_rate * t_r + half_rate
    a = jnp.exp(log_a)
    one_minus_a2 = -jnp.tanh(log_a) * (a * a + 1.0)
    bq = jnp.sqrt(one_minus_a2) * ((0.5 * t_i + 0.5) * xc2)

    a_g = a.reshape(nb * ng, SUBLANES, w)
    b_g = bq.reshape(nb * ng, SUBLANES, w)
    tpos = lax.broadcasted_iota(jnp.int32, a_g.shape, 1)
    for sh in (1, 2, 4):
        keep = tpos >= sh
        a_prev = jnp.where(keep, pltpu.roll(a_g, sh, 1), 1.0)
        b_prev = jnp.where(keep, pltpu.roll(b_g, sh, 1), 0.0)
        b_g = a_g * b_prev + b_g
        a_g = a_g * a_prev
    a_g = a_g.reshape(nb, ng, SUBLANES, w)
    b_g = b_g.reshape(nb, ng, SUBLANES, w)
    hprev = hcar_scr[...]
    for g in range(ng):
        hg = a_g[:, g] * hprev + b_g[:, g]
        hs_scr[:, g * SUBLANES:(g + 1) * SUBLANES, :] = hg
        hprev = jnp.broadcast_to(hg[:, SUBLANES - 1:SUBLANES, :], hg.shape)
    hcar_scr[...] = hprev
    h_out_ref[...] = hprev[:, 0:1, :]

    hs = hs_scr[...].reshape(rows, w)
    gated1 = (hs * _silu(z)).astype(BF16)
    out1 = _dot(gated1, w_ro_ref[...]).reshape(nb, tt, d)
    x2 = x1 + gate1_ref[...] * out1
    y_ref[...] = _rms(x2, gf_ref[...])


def _back(o, zg, x, mod0, mod1, conv0, h0, weights, *, nb, tt, o_is_value):
    bseq, t, d = x.shape
    w = conv0.shape[-1]
    if o_is_value:
        o_spec = pl.BlockSpec((nb, tt, d), lambda i, j: (i, j, 0))
    else:
        o_spec = pl.BlockSpec((nb, N_HEADS, tt, KV_LORA), lambda i, j: (i, 0, j, 0))
    in_specs = [
        o_spec,
        pl.BlockSpec((nb, tt, d), lambda i, j: (i, j, 0)),
        pl.BlockSpec((nb, tt, d), lambda i, j: (i, j, 0)),
        pl.BlockSpec((nb, 1, d), lambda i, j: (i, 0, 2)),
        pl.BlockSpec((nb, 1, d), lambda i, j: (i, 0, 0)),
        pl.BlockSpec((nb, 1, d), lambda i, j: (i, 0, 1)),
        pl.BlockSpec((nb, 1, d), lambda i, j: (i, 0, 2)),
        pl.BlockSpec((nb, CONV_W - 1, w), lambda i, j: (i, 0, 0)),
        pl.BlockSpec((nb, 1, w), lambda i, j: (i, 0, 0)),
    ] + [_full(wt.shape) for wt in weights]
    out_shape = (
        jax.ShapeDtypeStruct((bseq, t, d), F32),
        jax.ShapeDtypeStruct((bseq, CONV_W - 1, w), F32),
        jax.ShapeDtypeStruct((bseq, 1, w), F32),
    )
    out_specs = (
        pl.BlockSpec((nb, tt, d), lambda i, j: (i, j, 0)),
        pl.BlockSpec((nb, CONV_W - 1, w), lambda i, j: (i, 0, 0)),
        pl.BlockSpec((nb, 1, w), lambda i, j: (i, 0, 0)),
    )
    return pl.pallas_call(
        functools.partial(_back_kernel, nb=nb, tt=tt, o_is_value=o_is_value),
        out_shape=out_shape,
        grid=(bseq // nb, t // tt),
        in_specs=in_specs,
        out_specs=out_specs,
        scratch_shapes=[
            pltpu.VMEM((nb, SUBLANES, w), F32),
            pltpu.VMEM((nb, tt, w), F32),
            pltpu.VMEM((nb, SUBLANES, w), F32),
        ],
        compiler_params=_params("arbitrary", "arbitrary"),
        name="back_prompt" if o_is_value else "back_sample",
    )(o, zg, x, mod0, mod1, mod1, mod1, conv0, h0, *weights)


def _rope_tables(pos):
    half = QK_ROPE // 2
    inv = ROPE_THETA ** (-jnp.arange(half, dtype=F32) / half)
    ang = pos.astype(F32)[:, None] * inv[None, :]
    cos, sin = jnp.cos(ang), jnp.sin(ang)
    cos_t = jnp.tile(jnp.concatenate([cos, cos], axis=1), (1, N_HEADS))
    sin_t = jnp.tile(jnp.concatenate([-sin, sin], axis=1), (1, N_HEADS))
    return cos_t, sin_t


def _tiles(bseq, t):
    tt = min(ROW_TILE, t)
    nb = max(1, min(bseq, ROW_TILE // tt))
    assert t % tt == 0 and bseq % nb == 0 and tt % SUBLANES == 0
    return nb, tt


def kernel(x_prompt, x_sample, c_prompt, c_sample, cache_ckv, cache_kpe, page_table, state_conv, state_h,
           norm_g, w_ada, b_ada, final_g,
           a_w_in, a_g_q, a_w_qb, a_g_kv, a_w_uk, a_w_uv, a_w_o,
           r_w_in, r_conv_w, r_conv_b, r_w_a, r_b_a, r_w_x, r_b_x, r_lam, r_w_o):
    bp, tp, d = x_prompt.shape
    bs, ts, _ = x_sample.shape
    assert norm_g.shape[0] == 2 and a_w_in.shape[0] == 1 and r_w_in.shape[0] == 1
    w = r_lam.shape[-1]
    past_len = page_table.shape[1] * PAGE_SIZE

    w_in = a_w_in[0]
    c_kv, c_pe = Q_LORA + KV_LORA, Q_LORA + KV_LORA + QK_ROPE
    w_in_p = jnp.concatenate(
        [w_in[:, :c_kv], w_in[:, c_pe:], w_in[:, c_kv:c_pe], w_in[:, c_kv:c_pe]], axis=1).astype(BF16)
    w_qb = a_w_qb[0].reshape(Q_LORA, N_HEADS, QK_NOPE + QK_ROPE)
    w_qb_p = jnp.concatenate(
        [w_qb[:, :, :QK_NOPE].reshape(Q_LORA, N_HEADS * QK_NOPE),
         w_qb[:, :, QK_NOPE:].reshape(Q_LORA, N_HEADS * QK_ROPE)], axis=1).astype(BF16)
    w_uk_p = a_w_uk[0].transpose(1, 2, 0).astype(BF16)
    w_uv_p = a_w_uv[0].transpose(1, 0, 2).astype(BF16)
    row = lambda v: v.reshape(1, -1)
    front_w = (row(norm_g[0]), w_in_p, row(a_g_q[0]), w_qb_p, row(a_g_kv[0]), w_uk_p)
    back_w = (w_uv_p, a_w_o[0].astype(BF16), row(norm_g[1]), r_w_in[0].astype(BF16), r_conv_w[0],
              row(r_conv_b[0]), (0.5 * r_w_a[0]).astype(BF16), row(0.5 * r_b_a[0]), (0.5 * r_w_x[0]).astype(BF16), row(0.5 * r_b_x[0]),
              row(r_lam[0]), r_w_o[0].astype(BF16), row(final_g))

    mod = _modulation(jnp.concatenate([c_prompt, c_sample], axis=0), w_ada, b_ada)
    mod_p = [mod[i, :bp].reshape(bp, 1, 3 * d) for i in range(2)]
    mod_s = [mod[i, bp:].reshape(bs, 1, 3 * d) for i in range(2)]

    cos_p, sin_p = _rope_tables(jnp.arange(tp))
    cos_s, sin_s = _rope_tables(past_len + jnp.arange(ts))

    _, tt_p = _tiles(bp, tp)
    nb_s, tt_s = _tiles(bs, ts)

    qT_p, ckv_p, kpeT_p, ckvb_p, kpeb_p, ckvT_p, zg_p = _front_prompt(
        x_prompt, mod_p[0], cos_p, sin_p, front_w, tt=tt_p)
    qlat_s, qpe_s, ckv_s, kpe_s, zg_s = _front_sample(
        x_sample, mod_s[0], cos_s, sin_s, front_w, nb=nb_s, tt=tt_s)

    v_p = _attn_prompt(qT_p, ckvb_p, kpeb_p, ckvT_p, w_uv_p, tq=tt_p)
    o_s = _attn_sample(page_table, qlat_s, qpe_s, ckv_s, kpe_s, cache_ckv[0], jnp.swapaxes(cache_kpe[0], 1, 2))

    y_p, conv_p, h_p = _back(v_p, zg_p, x_prompt, mod_p[0], mod_p[1],
                             jnp.zeros((bp, CONV_W - 1, w), F32), jnp.zeros((bp, 1, w), F32),
                             back_w, nb=1, tt=tt_p, o_is_value=True)
    y_s, conv_s, h_s = _back(o_s, zg_s, x_sample, mod_s[0], mod_s[1],
                             state_conv[0], state_h[0].reshape(bs, 1, w),
                             back_w, nb=nb_s, tt=tt_s, o_is_value=False)

    return (y_p, y_s, ckv_p[None], jnp.swapaxes(kpeT_p, 1, 2)[None], ckv_s[None], kpe_s[None],
            conv_p[None], conv_s[None], h_p.reshape(1, bp, w), h_s.reshape(1, bs, w))
```

```python
import functools
import math

import jax
import jax.numpy as jnp
from jax import lax
from jax.experimental import pallas as pl
from jax.experimental.pallas import tpu as pltpu

F32 = jnp.float32
BF16 = jnp.bfloat16

N_HEADS = 8
QK_NOPE = 128
QK_ROPE = 64
V_DIM = 128
Q_LORA = 512
KV_LORA = 256
QK_CAT = KV_LORA + QK_ROPE
ROPE_THETA = 10000.0
SM_SCALE = 1.0 / math.sqrt(QK_NOPE + QK_ROPE)
Q_SCALE = SM_SCALE * math.log2(math.e)
NEG_INF = -1e30
LRU_BLOCKS = 4
CONV_W = 4
LRU_C = 8.0
NORM_EPS = 1e-6
PAGE_SIZE = 128

VMEM_LIMIT_BYTES = 56 * 1024 * 1024
SUBLANES = 8
ROW_TILE = 256
PROMPT_ROW_TILE = 512


def _dot(a, b):
    return jnp.dot(a, b, preferred_element_type=F32)


def _dot_nt(a, b):
    return lax.dot_general(a, b, (((1,), (1,)), ((), ())), preferred_element_type=F32)


def _rms(x, g):
    return x * lax.rsqrt(jnp.mean(x * x, axis=-1, keepdims=True) + NORM_EPS) * g


def _adaln(x, g, scale, shift):
    r = lax.rsqrt(jnp.mean(x * x, axis=-1, keepdims=True) + NORM_EPS)
    return x * r * (g * (1.0 + scale)) + shift


def _silu(x):
    half = 0.5 * x
    return half * jnp.tanh(half) + half


def _params(*sem):
    return pltpu.CompilerParams(dimension_semantics=sem, vmem_limit_bytes=VMEM_LIMIT_BYTES)


def _full(shape):
    return pl.BlockSpec(shape, lambda *_: (0,) * len(shape))


def _mod_kernel(c_ref, w_ref, b_ref, o_ref):
    o_ref[0] = _dot(c_ref[...].astype(BF16), w_ref[0].astype(BF16)) + b_ref[0]


def _modulation(c_all, w_ada, b_ada):
    depth, d, d3 = w_ada.shape
    n = c_all.shape[0]
    return pl.pallas_call(
        _mod_kernel,
        out_shape=jax.ShapeDtypeStruct((depth, n, d3), F32),
        grid=(depth, d3 // d),
        in_specs=[
            pl.BlockSpec((n, d), lambda i, j: (0, 0)),
            pl.BlockSpec((1, d, d), lambda i, j: (i, 0, j)),
            pl.BlockSpec((1, 1, d), lambda i, j: (i, 0, j)),
        ],
        out_specs=pl.BlockSpec((1, n, d), lambda i, j: (i, 0, j)),
        compiler_params=_params("arbitrary", "arbitrary"),
        name="mod",
    )(c_all, w_ada, b_ada.reshape(depth, 1, d3))


def _rope_rows(x, cos, sin_signed, nb, tt):
    width = x.shape[-1]
    lane = lax.broadcasted_iota(jnp.int32, x.shape, 1)
    first_half = (lane & (QK_ROPE - 1)) < (QK_ROPE // 2)
    partner = jnp.where(first_half,
                        pltpu.roll(x, width - QK_ROPE // 2, 1),
                        pltpu.roll(x, QK_ROPE // 2, 1))
    x3 = x.reshape(nb, tt, width)
    p3 = partner.reshape(nb, tt, width)
    return x3 * cos[None] + p3 * sin_signed[None]


def _front_compute(x_ref, shift_ref, scale_ref, cos_ref, sin_ref, g_ref, w_in_ref, gq_ref, w_qb_ref, gkv_ref,
                   nb, tt):
    rows = nb * tt
    d = x_ref.shape[-1]
    h = _adaln(x_ref[...], g_ref[...], scale_ref[...], shift_ref[...])
    hb = h.reshape(rows, d).astype(BF16)
    c0, c1, c2, c3 = Q_LORA, Q_LORA + KV_LORA, Q_LORA + KV_LORA + d, Q_LORA + KV_LORA + d + 2 * QK_ROPE
    u_q = _dot(hb, w_in_ref[:, 0:c0])
    u_kv = _dot(hb, w_in_ref[:, c0:c1])
    z = _dot(hb, w_in_ref[:, c1:c2])
    u_kpe = _dot(hb, w_in_ref[:, c2:c3])
    zg = _silu(z)
    ckv = _rms(u_kv, gkv_ref[...])
    cos = cos_ref[...]
    sin = sin_ref[...]
    kr = _rope_rows(u_kpe, cos[:, 0:2 * QK_ROPE], sin[:, 0:2 * QK_ROPE], nb, tt)
    qn = _rms(u_q, gq_ref[...]).astype(BF16)
    q = _dot(qn, w_qb_ref[...])
    return zg, ckv, kr, q, cos, sin


def _front_prompt_kernel(x_ref, shift_ref, scale_ref, cos_ref, sin_ref, g_ref, w_in_ref, gq_ref, w_qb_ref,
                         gkv_ref, w_uk_ref,
                         qT_ref, ckv_ref, kpeT_ref, ckvb_ref, kpeb_ref, ckvT_ref, zg_ref, *, tt, tq):
    nope_w = N_HEADS * QK_NOPE
    pe_w = N_HEADS * QK_ROPE
    zg, ckv, kr, q, cos, sin = _front_compute(
        x_ref, shift_ref, scale_ref, cos_ref, sin_ref, g_ref, w_in_ref, gq_ref, w_qb_ref, gkv_ref, 1, tt)
    zg_ref[0] = zg.astype(zg_ref.dtype)
    ckv_ref[0] = ckv
    ckvb_ref[0] = ckv.astype(ckvb_ref.dtype)
    ckvT = ckv.T.astype(ckvT_ref.dtype)
    kr2 = kr[0]
    kpeb_ref[0] = kr2[:, 0:QK_ROPE].astype(kpeb_ref.dtype)
    kpeT_ref[0] = kr2.T[0:QK_ROPE, :]
    tiles = [(s, slice(s * tq, (s + 1) * tq)) for s in range(tt // tq)]
    for s, tok in tiles:
        ckvT_ref[0, s] = ckvT[:, tok]
    qrT = (_rope_rows(q[:, nope_w:nope_w + pe_w], cos, sin, 1, tt)[0] * Q_SCALE).T
    for hd in range(N_HEADS):
        col = slice(hd * tq, (hd + 1) * tq)
        qh = q[:, hd * QK_NOPE:(hd + 1) * QK_NOPE].astype(BF16)
        qlT = (_dot(qh, w_uk_ref[hd]) * Q_SCALE).T
        for s, tok in tiles:
            qT_ref[0, s, 0:KV_LORA, col] = qlT[:, tok].astype(qT_ref.dtype)
            qT_ref[0, s, KV_LORA:QK_CAT, col] = qrT[hd * QK_ROPE:(hd + 1) * QK_ROPE, tok].astype(qT_ref.dtype)


def _front_sample_kernel(x_ref, shift_ref, scale_ref, cos_ref, sin_ref, g_ref, w_in_ref, gq_ref, w_qb_ref,
                         gkv_ref, w_uk_ref,
                         qlat_ref, qpe_ref, ckv_ref, kpe_ref, zg_ref, *, nb, tt):
    d = x_ref.shape[-1]
    nope_w = N_HEADS * QK_NOPE
    pe_w = N_HEADS * QK_ROPE
    zg, ckv, kr, q, cos, sin = _front_compute(
        x_ref, shift_ref, scale_ref, cos_ref, sin_ref, g_ref, w_in_ref, gq_ref, w_qb_ref, gkv_ref, nb, tt)
    zg_ref[...] = zg.reshape(nb, tt, d).astype(zg_ref.dtype)
    ckv_ref[...] = ckv.reshape(nb, tt, KV_LORA)
    kpe_ref[...] = kr[:, :, 0:QK_ROPE]
    for hd in range(N_HEADS):
        qh = q[:, hd * QK_NOPE:(hd + 1) * QK_NOPE].astype(BF16)
        ql = _dot(qh, w_uk_ref[hd]) * Q_SCALE
        qlat_ref[:, hd] = ql.reshape(nb, tt, KV_LORA)
    qr = _rope_rows(q[:, nope_w:nope_w + pe_w], cos, sin, nb, tt) * Q_SCALE
    for hd in range(N_HEADS):
        qpe_ref[:, hd] = qr[:, :, hd * QK_ROPE:(hd + 1) * QK_ROPE]


def _front_in_specs(nb, tt, d, weights):
    pe_w = N_HEADS * QK_ROPE
    return [
        pl.BlockSpec((nb, tt, d), lambda i, j: (i, j, 0)),
        pl.BlockSpec((nb, 1, d), lambda i, j: (i, 0, 0)),
        pl.BlockSpec((nb, 1, d), lambda i, j: (i, 0, 1)),
        pl.BlockSpec((tt, pe_w), lambda i, j: (j, 0)),
        pl.BlockSpec((tt, pe_w), lambda i, j: (j, 0)),
    ] + [_full(w.shape) for w in weights]


def _front_prompt(x, mod3, cos_t, sin_t, weights, *, tt, tq):
    b, t, d = x.shape
    assert tt % tq == 0 and t % tt == 0
    nq = tt // tq
    nt = t // tq
    out_shape = (
        jax.ShapeDtypeStruct((b, nt, QK_CAT, N_HEADS * tq), BF16),
        jax.ShapeDtypeStruct((b, t, KV_LORA), F32),
        jax.ShapeDtypeStruct((b, QK_ROPE, t), F32),
        jax.ShapeDtypeStruct((b, t, KV_LORA), BF16),
        jax.ShapeDtypeStruct((b, t, QK_ROPE), BF16),
        jax.ShapeDtypeStruct((b, nt, KV_LORA, tq), BF16),
        jax.ShapeDtypeStruct((b, t, d), BF16),
    )
    out_specs = (
        pl.BlockSpec((1, nq, QK_CAT, N_HEADS * tq), lambda i, j: (i, j, 0, 0)),
        pl.BlockSpec((1, tt, KV_LORA), lambda i, j: (i, j, 0)),
        pl.BlockSpec((1, QK_ROPE, tt), lambda i, j: (i, 0, j)),
        pl.BlockSpec((1, tt, KV_LORA), lambda i, j: (i, j, 0)),
        pl.BlockSpec((1, tt, QK_ROPE), lambda i, j: (i, j, 0)),
        pl.BlockSpec((1, nq, KV_LORA, tq), lambda i, j: (i, j, 0, 0)),
        pl.BlockSpec((1, tt, d), lambda i, j: (i, j, 0)),
    )
    return pl.pallas_call(
        functools.partial(_front_prompt_kernel, tt=tt, tq=tq),
        out_shape=out_shape,
        grid=(b, t // tt),
        in_specs=_front_in_specs(1, tt, d, weights),
        out_specs=out_specs,
        compiler_params=_params("arbitrary", "arbitrary"),
        name="front_prompt",
    )(x, mod3, mod3, cos_t, sin_t, *weights)


def _front_sample(x, mod3, cos_t, sin_t, weights, *, nb, tt):
    bseq, t, d = x.shape
    out_shape = (
        jax.ShapeDtypeStruct((bseq, N_HEADS, t, KV_LORA), F32),
        jax.ShapeDtypeStruct((bseq, N_HEADS, t, QK_ROPE), F32),
        jax.ShapeDtypeStruct((bseq, t, KV_LORA), F32),
        jax.ShapeDtypeStruct((bseq, t, QK_ROPE), F32),
        jax.ShapeDtypeStruct((bseq, t, d), BF16),
    )
    out_specs = (
        pl.BlockSpec((nb, N_HEADS, tt, KV_LORA), lambda i, j: (i, 0, j, 0)),
        pl.BlockSpec((nb, N_HEADS, tt, QK_ROPE), lambda i, j: (i, 0, j, 0)),
        pl.BlockSpec((nb, tt, KV_LORA), lambda i, j: (i, j, 0)),
        pl.BlockSpec((nb, tt, QK_ROPE), lambda i, j: (i, j, 0)),
        pl.BlockSpec((nb, tt, d), lambda i, j: (i, j, 0)),
    )
    return pl.pallas_call(
        functools.partial(_front_sample_kernel, nb=nb, tt=tt),
        out_shape=out_shape,
        grid=(bseq // nb, t // tt),
        in_specs=_front_in_specs(nb, tt, d, weights),
        out_specs=out_specs,
        compiler_params=_params("arbitrary", "arbitrary"),
        name="front_sample",
    )(x, mod3, mod3, cos_t, sin_t, *weights)


def _attn_prompt_kernel(qT_ref, k_ref, kp_ref, vT_ref, w_uv_ref, o_ref, m_scr, l_scr, acc_scr, sa_scr, sb_scr,
                        *, tq):
    qi = pl.program_id(1)
    qlT = qT_ref[0, 0, 0:KV_LORA, :]
    qpT = qT_ref[0, 0, KV_LORA:QK_CAT, :]

    m_scr[...] = jnp.full(m_scr.shape, NEG_INF, F32)
    l_scr[...] = jnp.zeros(l_scr.shape, F32)
    acc_scr[...] = jnp.zeros(acc_scr.shape, F32)

    def scores(j):
        start = pl.multiple_of(j * tq, tq)
        k = k_ref[0, pl.ds(start, tq), :]
        kp = kp_ref[0, pl.ds(start, tq), :]
        return _dot(k, qlT) + _dot(kp, qpT)

    def accumulate(j, sT, masked):
        if masked:
            kpos = lax.broadcasted_iota(jnp.int32, sT.shape, 0)
            qpos = lax.broadcasted_iota(jnp.int32, sT.shape, 1) & (tq - 1)
            sT = jnp.where(kpos <= qpos, sT, NEG_INF)
        m_prev = m_scr[...]
        m_new = jnp.maximum(m_prev, jnp.max(sT, axis=0, keepdims=True))
        alpha = jnp.exp2(m_prev - m_new)
        pT = jnp.exp2(sT - m_new)
        l_scr[...] = alpha * l_scr[...] + jnp.sum(pT, axis=0, keepdims=True)
        acc_scr[...] = alpha * acc_scr[...] + _dot(vT_ref[0, j], pT.astype(BF16))
        m_scr[...] = m_new

    sa_scr[...] = scores(0)

    def body(i, carry):
        j = 2 * i
        sb_scr[...] = scores(j + 1)
        accumulate(j, sa_scr[...], False)
        sa_scr[...] = scores(j + 2)
        accumulate(j + 1, sb_scr[...], False)
        return carry

    lax.fori_loop(0, qi // 2, body, 0)

    @pl.when(qi % 2 == 0)
    def _():
        accumulate(qi, sa_scr[...], True)

    @pl.when(qi % 2 == 1)
    def _():
        sb_scr[...] = scores(qi)
        accumulate(qi - 1, sa_scr[...], False)
        accumulate(qi, sb_scr[...], True)

    oT = acc_scr[...] / l_scr[...]
    for hd in range(N_HEADS):
        o_h = oT[:, hd * tq:(hd + 1) * tq].T.astype(BF16)
        o_ref[0, :, hd * V_DIM:(hd + 1) * V_DIM] = _dot(o_h, w_uv_ref[hd]).astype(o_ref.dtype)


def _attn_prompt(qT, ckvb, kpeb, ckvT, w_uv_p, *, tq):
    b, nt, _, cols = qT.shape
    t = nt * tq
    assert tq & (tq - 1) == 0
    return pl.pallas_call(
        functools.partial(_attn_prompt_kernel, tq=tq),
        out_shape=jax.ShapeDtypeStruct((b, t, N_HEADS * V_DIM), BF16),
        grid=(b, nt),
        in_specs=[
            pl.BlockSpec((1, 1, QK_CAT, cols), lambda i, j: (i, j, 0, 0)),
            pl.BlockSpec((1, t, KV_LORA), lambda i, j: (i, 0, 0)),
            pl.BlockSpec((1, t, QK_ROPE), lambda i, j: (i, 0, 0)),
            pl.BlockSpec((1, nt, KV_LORA, tq), lambda i, j: (i, 0, 0, 0)),
            _full(w_uv_p.shape),
        ],
        out_specs=pl.BlockSpec((1, tq, N_HEADS * V_DIM), lambda i, j: (i, j, 0)),
        scratch_shapes=[
            pltpu.VMEM((1, cols), F32),
            pltpu.VMEM((1, cols), F32),
            pltpu.VMEM((KV_LORA, cols), F32),
            pltpu.VMEM((tq, cols), F32),
            pltpu.VMEM((tq, cols), F32),
        ],
        compiler_params=_params("arbitrary", "arbitrary"),
        name="attn_prompt",
    )(qT, ckvb, kpeb, ckvT, w_uv_p)


SAMPLE_CHUNK_PAGES = 32
SAMPLE_SLOTS = 4


def _sample_stages(pt_ref, ql_ref, qp_ref, kn_ref, kpn_ref, cache_k, cache_pT, o_ref,
                   kbuf, pbuf, kb_scr, sems, *, step, n_steps, spp, n_chunks, cp, ts):
    rows = N_HEADS * ts
    nseq = n_steps * spp
    ahead = SAMPLE_SLOTS - 1

    def page_copies(seq, chunk, slot, page):
        phys = pt_ref[seq, chunk * cp + page]
        dst = pl.ds(page * PAGE_SIZE, PAGE_SIZE)
        return (
            pltpu.make_async_copy(cache_k.at[phys], kbuf.at[slot, dst], sems.at[slot, 0]),
            pltpu.make_async_copy(cache_pT.at[phys], pbuf.at[slot, :, dst], sems.at[slot, 1]),
        )

    def start_chunk(seq, chunk, slot):
        for page in range(cp):
            for c in page_copies(seq, chunk, slot, page):
                c.start()

    def wait_chunk(seq, chunk, slot):
        for page in range(cp):
            for c in page_copies(seq, chunk, slot, page):
                c.wait()

    def update(st, s, v):
        m_new = jnp.maximum(st["m"], jnp.max(s, axis=1, keepdims=True))
        alpha = jnp.exp2(st["m"] - m_new)
        p = jnp.exp2(s - m_new)
        st["l"] = alpha * st["l"] + jnp.sum(p, axis=1, keepdims=True)
        st["acc"] = alpha * st["acc"] + _dot(p.astype(BF16), v)
        st["m"] = m_new

    def make_stage(q, c, st):
        seq = step * spp + q

        def stage():
            if q == 0 and c == 0:
                @pl.when(step == 0)
                def _():
                    for c0 in range(ahead):
                        start_chunk(0, c0, c0 % SAMPLE_SLOTS)
            if c == 0:
                st["ql"] = ql_ref[q].reshape(rows, KV_LORA).astype(BF16)
                st["qp"] = qp_ref[q].reshape(rows, QK_ROPE).astype(BF16)
                st["m"] = jnp.full((rows, 1), NEG_INF, F32)
                st["l"] = jnp.zeros((rows, 1), F32)
                st["acc"] = jnp.zeros((rows, KV_LORA), F32)
            if c < n_chunks:
                slot = c % SAMPLE_SLOTS
                nxt = c + ahead
                nxt_slot = nxt % SAMPLE_SLOTS
                if nxt < n_chunks:
                    start_chunk(seq, nxt, nxt_slot)
                else:
                    @pl.when(seq + 1 < nseq)
                    def _():
                        start_chunk(seq + 1, nxt - n_chunks, nxt_slot)
                wait_chunk(seq, c, slot)
                k = kbuf[slot].astype(BF16)
                kb_scr[c % 2] = k
                s_cur = _dot_nt(st["ql"], k) + _dot(st["qp"], pbuf[slot].astype(BF16))
            if c == 0:
                kn = kn_ref[q].astype(BF16)
                kpn = kpn_ref[q].astype(BF16)
                s = _dot_nt(st["ql"], kn) + _dot_nt(st["qp"], kpn)
                s3 = s.reshape(N_HEADS, ts, ts)
                qpos = lax.broadcasted_iota(jnp.int32, s3.shape, 1)
                kpos = lax.broadcasted_iota(jnp.int32, s3.shape, 2)
                update(st, jnp.where(kpos <= qpos, s3, NEG_INF).reshape(rows, ts), kn)
            else:
                update(st, st["s_prev"], kb_scr[(c - 1) % 2])
            if c < n_chunks:
                st["s_prev"] = s_cur
            else:
                o_ref[q] = (st["acc"] / st["l"]).reshape(N_HEADS, ts, KV_LORA)

        return stage

    stages = []
    for q in range(spp):
        st = {}
        stages += [make_stage(q, c, st) for c in range(n_chunks + 1)]
    return stages


def _attn_sample_kernel(pt_ref, ql_ref, qp_ref, kn_ref, kpn_ref, cache_k, cache_pT, o_ref,
                        kbuf, pbuf, kb_scr, sems, *, n_chunks, cp, ts):
    for stage in _sample_stages(pt_ref, ql_ref, qp_ref, kn_ref, kpn_ref, cache_k, cache_pT, o_ref,
                                kbuf, pbuf, kb_scr, sems, step=pl.program_id(0), n_steps=pl.num_programs(0),
                                spp=1, n_chunks=n_chunks, cp=cp, ts=ts):
        stage()


def _attn_sample(page_table, qlat, qpe, ckv_new, kpe_new, cache_ckv, cache_kpeT):
    bs, _, ts, _ = qlat.shape
    n_pages = page_table.shape[1]
    cp = min(SAMPLE_CHUNK_PAGES, n_pages // SAMPLE_SLOTS)
    assert n_pages % cp == 0
    n_chunks = n_pages // cp
    assert n_chunks % SAMPLE_SLOTS == 0
    grid_spec = pltpu.PrefetchScalarGridSpec(
        num_scalar_prefetch=1,
        grid=(bs,),
        in_specs=[
            pl.BlockSpec((1, N_HEADS, ts, KV_LORA), lambda i, pt: (i, 0, 0, 0)),
            pl.BlockSpec((1, N_HEADS, ts, QK_ROPE), lambda i, pt: (i, 0, 0, 0)),
            pl.BlockSpec((1, ts, KV_LORA), lambda i, pt: (i, 0, 0)),
            pl.BlockSpec((1, ts, QK_ROPE), lambda i, pt: (i, 0, 0)),
            pl.BlockSpec(memory_space=pl.ANY),
            pl.BlockSpec(memory_space=pl.ANY),
        ],
        out_specs=pl.BlockSpec((1, N_HEADS, ts, KV_LORA), lambda i, pt: (i, 0, 0, 0)),
        scratch_shapes=[
            pltpu.VMEM((SAMPLE_SLOTS, cp * PAGE_SIZE, KV_LORA), F32),
            pltpu.VMEM((SAMPLE_SLOTS, QK_ROPE, cp * PAGE_SIZE), F32),
            pltpu.VMEM((2, cp * PAGE_SIZE, KV_LORA), BF16),
            pltpu.SemaphoreType.DMA((SAMPLE_SLOTS, 2)),
        ],
    )
    return pl.pallas_call(
        functools.partial(_attn_sample_kernel, n_chunks=n_chunks, cp=cp, ts=ts),
        out_shape=jax.ShapeDtypeStruct((bs, N_HEADS, ts, KV_LORA), F32),
        grid_spec=grid_spec,
        compiler_params=_params("arbitrary"),
        name="attn_sample",
    )(page_table, qlat, qpe, ckv_new, kpe_new, cache_ckv, cache_kpeT)


def _back_phases(o_ref, zg_ref, x_ref, gate0_ref, shift_ref, scale_ref, gate1_ref, conv0_ref, h0_ref,
                 w_uv_ref, w_o_ref, g1_ref, w_in_ref, conv_w_ref, conv_b_ref, w_a_ref, b_a_ref,
                 w_x_ref, b_x_ref, lam_ref, w_ro_ref, gf_ref,
                 y_ref, conv_out_ref, h_out_ref,
                 xprev_scr, hs_scr, hcar_scr, *, nb, tt, o_is_value):
    j = pl.program_id(1)
    rows = nb * tt
    d = x_ref.shape[-1]
    w = lam_ref.shape[-1]
    bw = w // LRU_BLOCKS
    ng = tt // SUBLANES
    st = {}

    def layer0_out():
        zg = zg_ref[...].reshape(rows, d)
        if o_is_value:
            gated = o_ref[...].reshape(rows, d) * zg
        else:
            v = jnp.concatenate(
                [_dot(o_ref[:, hd].reshape(rows, KV_LORA).astype(BF16), w_uv_ref[hd])
                 for hd in range(N_HEADS)], axis=1)
            gated = (v * zg.astype(F32)).astype(BF16)
        out0 = _dot(gated, w_o_ref[...]).reshape(nb, tt, d)
        st["x1"] = x_ref[...] + gate0_ref[...] * out0

    def lru_in():
        h1 = _adaln(st["x1"], g1_ref[...], scale_ref[...], shift_ref[...])
        hb = h1.reshape(rows, d).astype(BF16)
        st["xb"] = _dot(hb, w_in_ref[:, 0:w])
        st["z"] = _dot(hb, w_in_ref[:, w:2 * w])

    def init_state():
        @pl.when(j == 0)
        def _():
            xprev_scr[:, 0:SUBLANES - (CONV_W - 1), :] = jnp.zeros((nb, SUBLANES - (CONV_W - 1), w), F32)
            xprev_scr[:, SUBLANES - (CONV_W - 1):SUBLANES, :] = conv0_ref[...]
            hcar_scr[...] = jnp.broadcast_to(h0_ref[...], hcar_scr.shape)

    def conv():
        xb3 = st.pop("xb").reshape(nb, tt, w)
        ext = jnp.concatenate([xprev_scr[...], xb3], axis=1).reshape(nb * (ng + 1), SUBLANES, w)
        xprev_scr[...] = xb3[:, tt - SUBLANES:tt, :]
        conv_out_ref[...] = xb3[:, tt - (CONV_W - 1):tt, :]
        gshape = (nb, ng, SUBLANES, w)
        gpos = lax.broadcasted_iota(jnp.int32, gshape, 2)
        xc = conv_b_ref[...] + xb3.reshape(gshape) * conv_w_ref[CONV_W - 1:CONV_W, :]
        for sh in range(1, CONV_W):
            rolled = pltpu.roll(ext, sh, 1).reshape(nb, ng + 1, SUBLANES, w)
            shifted = jnp.where(gpos >= sh, rolled[:, 1:], rolled[:, :ng])
            xc = xc + shifted * conv_w_ref[CONV_W - 1 - sh:CONV_W - sh, :]
        st["xc"] = xc.reshape(rows, w)

    def gates():
        xc2 = st["xc"]
        xcb = xc2.astype(BF16)
        st["t_r"] = jnp.tanh(jnp.concatenate(
            [_dot(xcb[:, n * bw:(n + 1) * bw], w_a_ref[n]) for n in range(LRU_BLOCKS)], axis=1) + b_a_ref[...])
        st["t_i"] = jnp.tanh(jnp.concatenate(
            [_dot(xcb[:, n * bw:(n + 1) * bw], w_x_ref[n]) for n in range(LRU_BLOCKS)], axis=1) + b_x_ref[...])

    def recurrence_inputs():
        neg_lam = -lam_ref[...]
        softplus = jnp.maximum(neg_lam, 0.0) + jnp.log1p(jnp.exp(-jnp.abs(neg_lam)))
        half_rate = (-0.5 * LRU_C) * softplus
        log_a = half_rate * st.pop("t_r") + half_rate
        a = jnp.exp(log_a)
        one_minus_a2 = -jnp.tanh(log_a) * (a * a + 1.0)
        st["a"] = a
        st["b"] = jnp.sqrt(one_minus_a2) * ((0.5 * st.pop("t_i") + 0.5) * st.pop("xc"))

    def scan_groups():
        a_g = st.pop("a").reshape(nb * ng, SUBLANES, w)
        b_g = st.pop("b").reshape(nb * ng, SUBLANES, w)
        tpos = lax.broadcasted_iota(jnp.int32, a_g.shape, 1)
        for sh in (1, 2, 4):
            keep = tpos >= sh
            a_prev = jnp.where(keep, pltpu.roll(a_g, sh, 1), 1.0)
            b_prev = jnp.where(keep, pltpu.roll(b_g, sh, 1), 0.0)
            b_g = a_g * b_prev + b_g
            a_g = a_g * a_prev
        st["a_g"] = a_g.reshape(nb, ng, SUBLANES, w)
        st["b_g"] = b_g.reshape(nb, ng, SUBLANES, w)

    def scan_carry():
        a_g = st.pop("a_g")
        b_g = st.pop("b_g")
        hprev = hcar_scr[...]
        for g in range(ng):
            hg = a_g[:, g] * hprev + b_g[:, g]
            hs_scr[:, g * SUBLANES:(g + 1) * SUBLANES, :] = hg
            hprev = jnp.broadcast_to(hg[:, SUBLANES - 1:SUBLANES, :], hg.shape)
        hcar_scr[...] = hprev
        h_out_ref[...] = hprev[:, 0:1, :]

    def lru_out():
        hs = hs_scr[...].reshape(rows, w)
        gated1 = (hs * _silu(st.pop("z"))).astype(BF16)
        out1 = _dot(gated1, w_ro_ref[...]).reshape(nb, tt, d)
        x2 = st.pop("x1") + gate1_ref[...] * out1
        y_ref[...] = _rms(x2, gf_ref[...])

    return [init_state, layer0_out, lru_in, conv, gates, recurrence_inputs, scan_groups, scan_carry, lru_out]


def _back_kernel(*refs, nb, tt, o_is_value):
    for phase in _back_phases(*refs, nb=nb, tt=tt, o_is_value=o_is_value):
        phase()


def _back(o, zg, x, mod0, mod1, conv0, h0, weights, *, nb, tt, o_is_value):
    bseq, t, d = x.shape
    w = conv0.shape[-1]
    if o_is_value:
        o_spec = pl.BlockSpec((nb, tt, d), lambda i, j: (i, j, 0))
    else:
        o_spec = pl.BlockSpec((nb, N_HEADS, tt, KV_LORA), lambda i, j: (i, 0, j, 0))
    in_specs = [
        o_spec,
        pl.BlockSpec((nb, tt, d), lambda i, j: (i, j, 0)),
        pl.BlockSpec((nb, tt, d), lambda i, j: (i, j, 0)),
        pl.BlockSpec((nb, 1, d), lambda i, j: (i, 0, 2)),
        pl.BlockSpec((nb, 1, d), lambda i, j: (i, 0, 0)),
        pl.BlockSpec((nb, 1, d), lambda i, j: (i, 0, 1)),
        pl.BlockSpec((nb, 1, d), lambda i, j: (i, 0, 2)),
        pl.BlockSpec((nb, CONV_W - 1, w), lambda i, j: (i, 0, 0)),
        pl.BlockSpec((nb, 1, w), lambda i, j: (i, 0, 0)),
    ] + [_full(wt.shape) for wt in weights]
    out_shape = (
        jax.ShapeDtypeStruct((bseq, t, d), F32),
        jax.ShapeDtypeStruct((bseq, CONV_W - 1, w), F32),
        jax.ShapeDtypeStruct((bseq, 1, w), F32),
    )
    out_specs = (
        pl.BlockSpec((nb, tt, d), lambda i, j: (i, j, 0)),
        pl.BlockSpec((nb, CONV_W - 1, w), lambda i, j: (i, 0, 0)),
        pl.BlockSpec((nb, 1, w), lambda i, j: (i, 0, 0)),
    )
    return pl.pallas_call(
        functools.partial(_back_kernel, nb=nb, tt=tt, o_is_value=o_is_value),
        out_shape=out_shape,
        grid=(bseq // nb, t // tt),
        in_specs=in_specs,
        out_specs=out_specs,
        scratch_shapes=[
            pltpu.VMEM((nb, SUBLANES, w), F32),
            pltpu.VMEM((nb, tt, w), F32),
            pltpu.VMEM((nb, SUBLANES, w), F32),
        ],
        compiler_params=_params("arbitrary", "arbitrary"),
        name="back_prompt" if o_is_value else "back_sample",
    )(o, zg, x, mod0, mod1, mod1, mod1, conv0, h0, *weights)


def _rope_tables(pos):
    half = QK_ROPE // 2
    inv = ROPE_THETA ** (-jnp.arange(half, dtype=F32) / half)
    ang = pos.astype(F32)[:, None] * inv[None, :]
    cos, sin = jnp.cos(ang), jnp.sin(ang)
    cos_t = jnp.tile(jnp.concatenate([cos, cos], axis=1), (1, N_HEADS))
    sin_t = jnp.tile(jnp.concatenate([-sin, sin], axis=1), (1, N_HEADS))
    return cos_t, sin_t


def _tiles(bseq, t):
    tt = min(ROW_TILE, t)
    nb = max(1, min(bseq, ROW_TILE // tt))
    assert t % tt == 0 and bseq % nb == 0 and tt % SUBLANES == 0
    return nb, tt


def kernel(x_prompt, x_sample, c_prompt, c_sample, cache_ckv, cache_kpe, page_table, state_conv, state_h,
           norm_g, w_ada, b_ada, final_g,
           a_w_in, a_g_q, a_w_qb, a_g_kv, a_w_uk, a_w_uv, a_w_o,
           r_w_in, r_conv_w, r_conv_b, r_w_a, r_b_a, r_w_x, r_b_x, r_lam, r_w_o):
    bp, tp, d = x_prompt.shape
    bs, ts, _ = x_sample.shape
    assert norm_g.shape[0] == 2 and a_w_in.shape[0] == 1 and r_w_in.shape[0] == 1
    w = r_lam.shape[-1]
    past_len = page_table.shape[1] * PAGE_SIZE

    w_in = a_w_in[0]
    c_kv, c_pe = Q_LORA + KV_LORA, Q_LORA + KV_LORA + QK_ROPE
    w_in_p = jnp.concatenate(
        [w_in[:, :c_kv], w_in[:, c_pe:], w_in[:, c_kv:c_pe], w_in[:, c_kv:c_pe]], axis=1).astype(BF16)
    w_qb = a_w_qb[0].reshape(Q_LORA, N_HEADS, QK_NOPE + QK_ROPE)
    w_qb_p = jnp.concatenate(
        [w_qb[:, :, :QK_NOPE].reshape(Q_LORA, N_HEADS * QK_NOPE),
         w_qb[:, :, QK_NOPE:].reshape(Q_LORA, N_HEADS * QK_ROPE)], axis=1).astype(BF16)
    w_uk_p = a_w_uk[0].transpose(1, 2, 0).astype(BF16)
    w_uv_p = a_w_uv[0].transpose(1, 0, 2).astype(BF16)
    row = lambda v: v.reshape(1, -1)
    front_w = (row(norm_g[0]), w_in_p, row(a_g_q[0]), w_qb_p, row(a_g_kv[0]), w_uk_p)
    back_w = (w_uv_p, a_w_o[0].astype(BF16), row(norm_g[1]), r_w_in[0].astype(BF16), r_conv_w[0],
              row(r_conv_b[0]), (0.5 * r_w_a[0]).astype(BF16), row(0.5 * r_b_a[0]),
              (0.5 * r_w_x[0]).astype(BF16), row(0.5 * r_b_x[0]),
              row(r_lam[0]), r_w_o[0].astype(BF16), row(final_g))

    mod = _modulation(jnp.concatenate([c_prompt, c_sample], axis=0), w_ada, b_ada)
    mod_p = [mod[i, :bp].reshape(bp, 1, 3 * d) for i in range(2)]
    mod_s = [mod[i, bp:].reshape(bs, 1, 3 * d) for i in range(2)]

    cos_p, sin_p = _rope_tables(jnp.arange(tp))
    cos_s, sin_s = _rope_tables(past_len + jnp.arange(ts))

    _, tq_p = _tiles(bp, tp)
    tt_p = min(PROMPT_ROW_TILE, tp)
    nb_s, tt_s = _tiles(bs, ts)

    qT_p, ckv_p, kpeT_p, ckvb_p, kpeb_p, ckvT_p, zg_p = _front_prompt(
        x_prompt, mod_p[0], cos_p, sin_p, front_w, tt=tt_p, tq=tq_p)
    qlat_s, qpe_s, ckv_s, kpe_s, zg_s = _front_sample(
        x_sample, mod_s[0], cos_s, sin_s, front_w, nb=nb_s, tt=tt_s)

    v_p = _attn_prompt(qT_p, ckvb_p, kpeb_p, ckvT_p, w_uv_p, tq=tq_p)

    o_s = _attn_sample(page_table, qlat_s, qpe_s, ckv_s, kpe_s, cache_ckv[0], jnp.swapaxes(cache_kpe[0], 1, 2))

    y_p, conv_p, h_p = _back(v_p, zg_p, x_prompt, mod_p[0], mod_p[1],
                             jnp.zeros((bp, CONV_W - 1, w), F32), jnp.zeros((bp, 1, w), F32),
                             back_w, nb=1, tt=tt_p, o_is_value=True)
    y_s, conv_s, h_s = _back(o_s, zg_s, x_sample, mod_s[0], mod_s[1],
                             state_conv[0], state_h[0].reshape(bs, 1, w),
                             back_w, nb=nb_s, tt=tt_s, o_is_value=False)

    return (y_p, y_s, ckv_p[None], jnp.swapaxes(kpeT_p, 1, 2)[None], ckv_s[None], kpe_s[None],
            conv_p[None], conv_s[None], h_p.reshape(1, bp, w), h_s.reshape(1, bs, w))
```

```python
import functools
import math

import jax
import jax.numpy as jnp
from jax import lax
from jax.experimental import pallas as pl
from jax.experimental.pallas import tpu as pltpu

F32 = jnp.float32
BF16 = jnp.bfloat16

N_HEADS = 8
QK_NOPE = 128
QK_ROPE = 64
V_DIM = 128
Q_LORA = 512
KV_LORA = 256
QK_CAT = KV_LORA + QK_ROPE
ROPE_THETA = 10000.0
SM_SCALE = 1.0 / math.sqrt(QK_NOPE + QK_ROPE)
Q_SCALE = SM_SCALE * math.log2(math.e)
NEG_INF = -1e30
LRU_BLOCKS = 4
CONV_W = 4
LRU_C = 8.0
NORM_EPS = 1e-6
PAGE_SIZE = 128

VMEM_LIMIT_BYTES = 56 * 1024 * 1024
SUBLANES = 8
ROW_TILE = 256
PROMPT_ROW_TILE = 512


def _dot(a, b):
    return jnp.dot(a, b, preferred_element_type=F32)


def _dot_nt(a, b):
    return lax.dot_general(a, b, (((1,), (1,)), ((), ())), preferred_element_type=F32)


def _rms(x, g):
    return x * lax.rsqrt(jnp.mean(x * x, axis=-1, keepdims=True) + NORM_EPS) * g


def _adaln(x, g, scale, shift):
    r = lax.rsqrt(jnp.mean(x * x, axis=-1, keepdims=True) + NORM_EPS)
    return x * r * (g * (1.0 + scale)) + shift


def _silu(x):
    half = 0.5 * x
    return half * jnp.tanh(half) + half


def _params(*sem):
    return pltpu.CompilerParams(dimension_semantics=sem, vmem_limit_bytes=VMEM_LIMIT_BYTES)


def _full(shape):
    return pl.BlockSpec(shape, lambda *_: (0,) * len(shape))


def _mod_kernel(c_ref, w_ref, b_ref, o_ref):
    o_ref[0] = _dot(c_ref[...].astype(BF16), w_ref[0].astype(BF16)) + b_ref[0]


def _modulation(c_all, w_ada, b_ada):
    depth, d, d3 = w_ada.shape
    n = c_all.shape[0]
    return pl.pallas_call(
        _mod_kernel,
        out_shape=jax.ShapeDtypeStruct((depth, n, d3), F32),
        grid=(depth, d3 // d),
        in_specs=[
            pl.BlockSpec((n, d), lambda i, j: (0, 0)),
            pl.BlockSpec((1, d, d), lambda i, j: (i, 0, j)),
            pl.BlockSpec((1, 1, d), lambda i, j: (i, 0, j)),
        ],
        out_specs=pl.BlockSpec((1, n, d), lambda i, j: (i, 0, j)),
        compiler_params=_params("arbitrary", "arbitrary"),
        name="mod",
    )(c_all, w_ada, b_ada.reshape(depth, 1, d3))


def _rope_rows(x, cos, sin_signed, nb, tt):
    width = x.shape[-1]
    lane = lax.broadcasted_iota(jnp.int32, x.shape, 1)
    first_half = (lane & (QK_ROPE - 1)) < (QK_ROPE // 2)
    partner = jnp.where(first_half,
                        pltpu.roll(x, width - QK_ROPE // 2, 1),
                        pltpu.roll(x, QK_ROPE // 2, 1))
    x3 = x.reshape(nb, tt, width)
    p3 = partner.reshape(nb, tt, width)
    return x3 * cos[None] + p3 * sin_signed[None]


def _front_compute(x_ref, shift_ref, scale_ref, cos_ref, sin_ref, g_ref, w_in_ref, gq_ref, w_qb_ref, gkv_ref,
                   nb, tt):
    rows = nb * tt
    d = x_ref.shape[-1]
    h = _adaln(x_ref[...], g_ref[...], scale_ref[...], shift_ref[...])
    hb = h.reshape(rows, d).astype(BF16)
    c0, c1, c2, c3 = Q_LORA, Q_LORA + KV_LORA, Q_LORA + KV_LORA + d, Q_LORA + KV_LORA + d + 2 * QK_ROPE
    u_q = _dot(hb, w_in_ref[:, 0:c0])
    u_kv = _dot(hb, w_in_ref[:, c0:c1])
    z = _dot(hb, w_in_ref[:, c1:c2])
    u_kpe = _dot(hb, w_in_ref[:, c2:c3])
    zg = _silu(z)
    ckv = _rms(u_kv, gkv_ref[...])
    cos = cos_ref[...]
    sin = sin_ref[...]
    kr = _rope_rows(u_kpe, cos[:, 0:2 * QK_ROPE], sin[:, 0:2 * QK_ROPE], nb, tt)
    qn = _rms(u_q, gq_ref[...]).astype(BF16)
    q = _dot(qn, w_qb_ref[...])
    return zg, ckv, kr, q, cos, sin


def _front_prompt_kernel(x_ref, shift_ref, scale_ref, cos_ref, sin_ref, g_ref, w_in_ref, gq_ref, w_qb_ref,
                         gkv_ref, w_uk_ref,
                         qT_ref, ckv_ref, kpeT_ref, ckvb_ref, kpeb_ref, ckvT_ref, zg_ref, *, tt, tq):
    nope_w = N_HEADS * QK_NOPE
    pe_w = N_HEADS * QK_ROPE
    zg, ckv, kr, q, cos, sin = _front_compute(
        x_ref, shift_ref, scale_ref, cos_ref, sin_ref, g_ref, w_in_ref, gq_ref, w_qb_ref, gkv_ref, 1, tt)
    zg_ref[0] = zg.astype(zg_ref.dtype)
    ckv_ref[0] = ckv
    ckvb_ref[0] = ckv.astype(ckvb_ref.dtype)
    ckvT = ckv.T.astype(ckvT_ref.dtype)
    kr2 = kr[0]
    kpeb_ref[0] = kr2[:, 0:QK_ROPE].astype(kpeb_ref.dtype)
    kpeT_ref[0] = kr2.T[0:QK_ROPE, :]
    tiles = [(s, slice(s * tq, (s + 1) * tq)) for s in range(tt // tq)]
    for s, tok in tiles:
        ckvT_ref[0, s] = ckvT[:, tok]
    qrT = (_rope_rows(q[:, nope_w:nope_w + pe_w], cos, sin, 1, tt)[0] * Q_SCALE).T
    for hd in range(N_HEADS):
        col = slice(hd * tq, (hd + 1) * tq)
        qh = q[:, hd * QK_NOPE:(hd + 1) * QK_NOPE].astype(BF16)
        qlT = (_dot(qh, w_uk_ref[hd]) * Q_SCALE).T
        for s, tok in tiles:
            qT_ref[0, s, 0:KV_LORA, col] = qlT[:, tok].astype(qT_ref.dtype)
            qT_ref[0, s, KV_LORA:QK_CAT, col] = qrT[hd * QK_ROPE:(hd + 1) * QK_ROPE, tok].astype(qT_ref.dtype)


def _front_sample_kernel(x_ref, shift_ref, scale_ref, cos_ref, sin_ref, g_ref, w_in_ref, gq_ref, w_qb_ref,
                         gkv_ref, w_uk_ref,
                         qlat_ref, qpe_ref, ckv_ref, kpe_ref, zg_ref, *, nb, tt):
    d = x_ref.shape[-1]
    nope_w = N_HEADS * QK_NOPE
    pe_w = N_HEADS * QK_ROPE
    zg, ckv, kr, q, cos, sin = _front_compute(
        x_ref, shift_ref, scale_ref, cos_ref, sin_ref, g_ref, w_in_ref, gq_ref, w_qb_ref, gkv_ref, nb, tt)
    zg_ref[...] = zg.reshape(nb, tt, d).astype(zg_ref.dtype)
    ckv_ref[...] = ckv.reshape(nb, tt, KV_LORA)
    kpe_ref[...] = kr[:, :, 0:QK_ROPE]
    for hd in range(N_HEADS):
        qh = q[:, hd * QK_NOPE:(hd + 1) * QK_NOPE].astype(BF16)
        ql = _dot(qh, w_uk_ref[hd]) * Q_SCALE
        qlat_ref[:, hd] = ql.reshape(nb, tt, KV_LORA)
    qr = _rope_rows(q[:, nope_w:nope_w + pe_w], cos, sin, nb, tt) * Q_SCALE
    for hd in range(N_HEADS):
        qpe_ref[:, hd] = qr[:, :, hd * QK_ROPE:(hd + 1) * QK_ROPE]


def _front_in_specs(nb, tt, d, weights):
    pe_w = N_HEADS * QK_ROPE
    return [
        pl.BlockSpec((nb, tt, d), lambda i, j: (i, j, 0)),
        pl.BlockSpec((nb, 1, d), lambda i, j: (i, 0, 0)),
        pl.BlockSpec((nb, 1, d), lambda i, j: (i, 0, 1)),
        pl.BlockSpec((tt, pe_w), lambda i, j: (j, 0)),
        pl.BlockSpec((tt, pe_w), lambda i, j: (j, 0)),
    ] + [_full(w.shape) for w in weights]


def _front_prompt(x, mod3, cos_t, sin_t, weights, *, tt, tq):
    b, t, d = x.shape
    assert tt % tq == 0 and t % tt == 0
    nq = tt // tq
    nt = t // tq
    out_shape = (
        jax.ShapeDtypeStruct((b, nt, QK_CAT, N_HEADS * tq), BF16),
        jax.ShapeDtypeStruct((b, t, KV_LORA), F32),
        jax.ShapeDtypeStruct((b, QK_ROPE, t), F32),
        jax.ShapeDtypeStruct((b, t, KV_LORA), BF16),
        jax.ShapeDtypeStruct((b, t, QK_ROPE), BF16),
        jax.ShapeDtypeStruct((b, nt, KV_LORA, tq), BF16),
        jax.ShapeDtypeStruct((b, t, d), BF16),
    )
    out_specs = (
        pl.BlockSpec((1, nq, QK_CAT, N_HEADS * tq), lambda i, j: (i, j, 0, 0)),
        pl.BlockSpec((1, tt, KV_LORA), lambda i, j: (i, j, 0)),
        pl.BlockSpec((1, QK_ROPE, tt), lambda i, j: (i, 0, j)),
        pl.BlockSpec((1, tt, KV_LORA), lambda i, j: (i, j, 0)),
        pl.BlockSpec((1, tt, QK_ROPE), lambda i, j: (i, j, 0)),
        pl.BlockSpec((1, nq, KV_LORA, tq), lambda i, j: (i, j, 0, 0)),
        pl.BlockSpec((1, tt, d), lambda i, j: (i, j, 0)),
    )
    return pl.pallas_call(
        functools.partial(_front_prompt_kernel, tt=tt, tq=tq),
        out_shape=out_shape,
        grid=(b, t // tt),
        in_specs=_front_in_specs(1, tt, d, weights),
        out_specs=out_specs,
        compiler_params=_params("arbitrary", "arbitrary"),
        name="front_prompt",
    )(x, mod3, mod3, cos_t, sin_t, *weights)


def _front_sample(x, mod3, cos_t, sin_t, weights, *, nb, tt):
    bseq, t, d = x.shape
    out_shape = (
        jax.ShapeDtypeStruct((bseq, N_HEADS, t, KV_LORA), F32),
        jax.ShapeDtypeStruct((bseq, N_HEADS, t, QK_ROPE), F32),
        jax.ShapeDtypeStruct((bseq, t, KV_LORA), F32),
        jax.ShapeDtypeStruct((bseq, t, QK_ROPE), F32),
        jax.ShapeDtypeStruct((bseq, t, d), BF16),
    )
    out_specs = (
        pl.BlockSpec((nb, N_HEADS, tt, KV_LORA), lambda i, j: (i, 0, j, 0)),
        pl.BlockSpec((nb, N_HEADS, tt, QK_ROPE), lambda i, j: (i, 0, j, 0)),
        pl.BlockSpec((nb, tt, KV_LORA), lambda i, j: (i, j, 0)),
        pl.BlockSpec((nb, tt, QK_ROPE), lambda i, j: (i, j, 0)),
        pl.BlockSpec((nb, tt, d), lambda i, j: (i, j, 0)),
    )
    return pl.pallas_call(
        functools.partial(_front_sample_kernel, nb=nb, tt=tt),
        out_shape=out_shape,
        grid=(bseq // nb, t // tt),
        in_specs=_front_in_specs(nb, tt, d, weights),
        out_specs=out_specs,
        compiler_params=_params("arbitrary", "arbitrary"),
        name="front_sample",
    )(x, mod3, mod3, cos_t, sin_t, *weights)


def _attn_prompt_kernel(qT_ref, k_ref, kp_ref, vT_ref, w_uv_ref, o_ref, m_scr, l_scr, acc_scr, sa_scr, sb_scr,
                        *, tq):
    qi = pl.program_id(1)
    qlT = qT_ref[0, 0, 0:KV_LORA, :]
    qpT = qT_ref[0, 0, KV_LORA:QK_CAT, :]

    m_scr[...] = jnp.full(m_scr.shape, NEG_INF, F32)
    l_scr[...] = jnp.zeros(l_scr.shape, F32)
    acc_scr[...] = jnp.zeros(acc_scr.shape, F32)

    def scores(j):
        start = pl.multiple_of(j * tq, tq)
        k = k_ref[0, pl.ds(start, tq), :]
        kp = kp_ref[0, pl.ds(start, tq), :]
        return _dot(k, qlT) + _dot(kp, qpT)

    def accumulate(j, sT, masked):
        if masked:
            kpos = lax.broadcasted_iota(jnp.int32, sT.shape, 0)
            qpos = lax.broadcasted_iota(jnp.int32, sT.shape, 1) & (tq - 1)
            sT = jnp.where(kpos <= qpos, sT, NEG_INF)
        m_prev = m_scr[...]
        m_new = jnp.maximum(m_prev, jnp.max(sT, axis=0, keepdims=True))
        alpha = jnp.exp2(m_prev - m_new)
        pT = jnp.exp2(sT - m_new)
        l_scr[...] = alpha * l_scr[...] + jnp.sum(pT, axis=0, keepdims=True)
        acc_scr[...] = alpha * acc_scr[...] + _dot(vT_ref[0, j], pT.astype(BF16))
        m_scr[...] = m_new

    sa_scr[...] = scores(0)

    def body(i, carry):
        j = 2 * i
        sb_scr[...] = scores(j + 1)
        accumulate(j, sa_scr[...], False)
        sa_scr[...] = scores(j + 2)
        accumulate(j + 1, sb_scr[...], False)
        return carry

    lax.fori_loop(0, qi // 2, body, 0)

    @pl.when(qi % 2 == 0)
    def _():
        accumulate(qi, sa_scr[...], True)

    @pl.when(qi % 2 == 1)
    def _():
        sb_scr[...] = scores(qi)
        accumulate(qi - 1, sa_scr[...], False)
        accumulate(qi, sb_scr[...], True)

    oT = acc_scr[...] * (1.0 / l_scr[...])
    for hd in range(N_HEADS):
        o_h = oT[:, hd * tq:(hd + 1) * tq].T.astype(BF16)
        o_ref[0, :, hd * V_DIM:(hd + 1) * V_DIM] = _dot(o_h, w_uv_ref[hd]).astype(o_ref.dtype)


def _attn_prompt(qT, ckvb, kpeb, ckvT, w_uv_p, *, tq):
    b, nt, _, cols = qT.shape
    t = nt * tq
    assert tq & (tq - 1) == 0
    return pl.pallas_call(
        functools.partial(_attn_prompt_kernel, tq=tq),
        out_shape=jax.ShapeDtypeStruct((b, t, N_HEADS * V_DIM), BF16),
        grid=(b, nt),
        in_specs=[
            pl.BlockSpec((1, 1, QK_CAT, cols), lambda i, j: (i, j, 0, 0)),
            pl.BlockSpec((1, t, KV_LORA), lambda i, j: (i, 0, 0)),
            pl.BlockSpec((1, t, QK_ROPE), lambda i, j: (i, 0, 0)),
            pl.BlockSpec((1, nt, KV_LORA, tq), lambda i, j: (i, 0, 0, 0)),
            _full(w_uv_p.shape),
        ],
        out_specs=pl.BlockSpec((1, tq, N_HEADS * V_DIM), lambda i, j: (i, j, 0)),
        scratch_shapes=[
            pltpu.VMEM((1, cols), F32),
            pltpu.VMEM((1, cols), F32),
            pltpu.VMEM((KV_LORA, cols), F32),
            pltpu.VMEM((tq, cols), F32),
            pltpu.VMEM((tq, cols), F32),
        ],
        compiler_params=_params("arbitrary", "arbitrary"),
        name="attn_prompt",
    )(qT, ckvb, kpeb, ckvT, w_uv_p)


SAMPLE_CHUNK_PAGES = 64
SAMPLE_SLOTS = 3


def _sample_stages(pt_ref, ql_ref, qp_ref, kn_ref, kpn_ref, cache_k, cache_pT, o_ref,
                   kbuf, pbuf, kb_scr, sems, *, step, n_steps, spp, n_chunks, cp, ts):
    rows = N_HEADS * ts
    nseq = n_steps * spp
    ahead = SAMPLE_SLOTS - 1

    def page_copies(seq, chunk, slot, page):
        phys = pt_ref[seq, chunk * cp + page]
        dst = pl.ds(page * PAGE_SIZE, PAGE_SIZE)
        return (
            pltpu.make_async_copy(cache_k.at[phys], kbuf.at[slot, dst], sems.at[slot, 0]),
            pltpu.make_async_copy(cache_pT.at[phys], pbuf.at[slot, :, dst], sems.at[slot, 1]),
        )

    def start_chunk(seq, chunk, slot):
        for page in range(cp):
            for c in page_copies(seq, chunk, slot, page):
                c.start()

    def wait_chunk(seq, chunk, slot):
        for page in range(cp):
            for c in page_copies(seq, chunk, slot, page):
                c.wait()

    def update(st, s, v):
        m_new = jnp.maximum(st["m"], jnp.max(s, axis=1, keepdims=True))
        alpha = jnp.exp2(st["m"] - m_new)
        p = jnp.exp2(s - m_new)
        st["l"] = alpha * st["l"] + jnp.sum(p, axis=1, keepdims=True)
        st["acc"] = alpha * st["acc"] + _dot(p.astype(BF16), v)
        st["m"] = m_new

    def make_stage(q, c, st):
        seq = step * spp + q

        def stage():
            if q == 0 and c == 0:
                @pl.when(step == 0)
                def _():
                    for c0 in range(ahead):
                        start_chunk(0, c0, c0 % SAMPLE_SLOTS)
            if c == 0:
                st["ql"] = ql_ref[q].reshape(rows, KV_LORA).astype(BF16)
                st["qp"] = qp_ref[q].reshape(rows, QK_ROPE).astype(BF16)
                st["m"] = jnp.full((rows, 1), NEG_INF, F32)
                st["l"] = jnp.zeros((rows, 1), F32)
                st["acc"] = jnp.zeros((rows, KV_LORA), F32)
            if c < n_chunks:
                slot = lax.rem(seq * n_chunks + c, SAMPLE_SLOTS)
                nxt = c + ahead
                nxt_slot = lax.rem(seq * n_chunks + nxt, SAMPLE_SLOTS)
                if nxt < n_chunks:
                    start_chunk(seq, nxt, nxt_slot)
                else:
                    @pl.when(seq + 1 < nseq)
                    def _():
                        start_chunk(seq + 1, nxt - n_chunks, nxt_slot)
                wait_chunk(seq, c, slot)
                k = kbuf[slot].astype(BF16)
                kb_scr[c % 2] = k
                s_cur = _dot_nt(st["ql"], k) + _dot(st["qp"], pbuf[slot].astype(BF16))
            if c == 0:
                kn = kn_ref[q].astype(BF16)
                kpn = kpn_ref[q].astype(BF16)
                s = _dot_nt(st["ql"], kn) + _dot_nt(st["qp"], kpn)
                s3 = s.reshape(N_HEADS, ts, ts)
                qpos = lax.broadcasted_iota(jnp.int32, s3.shape, 1)
                kpos = lax.broadcasted_iota(jnp.int32, s3.shape, 2)
                update(st, jnp.where(kpos <= qpos, s3, NEG_INF).reshape(rows, ts), kn)
            else:
                update(st, st["s_prev"], kb_scr[(c - 1) % 2])
            if c < n_chunks:
                st["s_prev"] = s_cur
            else:
                o_ref[q] = (st["acc"] / st["l"]).reshape(N_HEADS, ts, KV_LORA)

        return stage

    stages = []
    for q in range(spp):
        st = {}
        stages += [make_stage(q, c, st) for c in range(n_chunks + 1)]
    return stages


def _attn_sample_kernel(pt_ref, ql_ref, qp_ref, kn_ref, kpn_ref, cache_k, cache_pT, o_ref,
                        kbuf, pbuf, kb_scr, sems, *, n_chunks, cp, ts):
    for stage in _sample_stages(pt_ref, ql_ref, qp_ref, kn_ref, kpn_ref, cache_k, cache_pT, o_ref,
                                kbuf, pbuf, kb_scr, sems, step=pl.program_id(0), n_steps=pl.num_programs(0),
                                spp=1, n_chunks=n_chunks, cp=cp, ts=ts):
        stage()


def _attn_sample(page_table, qlat, qpe, ckv_new, kpe_new, cache_ckv, cache_kpeT):
    bs, _, ts, _ = qlat.shape
    n_pages = page_table.shape[1]
    cp = min(SAMPLE_CHUNK_PAGES, n_pages // 2)
    assert n_pages % cp == 0
    n_chunks = n_pages // cp
    assert n_chunks >= SAMPLE_SLOTS - 1
    grid_spec = pltpu.PrefetchScalarGridSpec(
        num_scalar_prefetch=1,
        grid=(bs,),
        in_specs=[
            pl.BlockSpec((1, N_HEADS, ts, KV_LORA), lambda i, pt: (i, 0, 0, 0)),
            pl.BlockSpec((1, N_HEADS, ts, QK_ROPE), lambda i, pt: (i, 0, 0, 0)),
            pl.BlockSpec((1, ts, KV_LORA), lambda i, pt: (i, 0, 0)),
            pl.BlockSpec((1, ts, QK_ROPE), lambda i, pt: (i, 0, 0)),
            pl.BlockSpec(memory_space=pl.ANY),
            pl.BlockSpec(memory_space=pl.ANY),
        ],
        out_specs=pl.BlockSpec((1, N_HEADS, ts, KV_LORA), lambda i, pt: (i, 0, 0, 0)),
        scratch_shapes=[
            pltpu.VMEM((SAMPLE_SLOTS, cp * PAGE_SIZE, KV_LORA), F32),
            pltpu.VMEM((SAMPLE_SLOTS, QK_ROPE, cp * PAGE_SIZE), F32),
            pltpu.VMEM((2, cp * PAGE_SIZE, KV_LORA), BF16),
            pltpu.SemaphoreType.DMA((SAMPLE_SLOTS, 2)),
        ],
    )
    return pl.pallas_call(
        functools.partial(_attn_sample_kernel, n_chunks=n_chunks, cp=cp, ts=ts),
        out_shape=jax.ShapeDtypeStruct((bs, N_HEADS, ts, KV_LORA), F32),
        grid_spec=grid_spec,
        compiler_params=_params("arbitrary"),
        name="attn_sample",
    )(page_table, qlat, qpe, ckv_new, kpe_new, cache_ckv, cache_kpeT)


def _back_phases(o_ref, zg_ref, x_ref, gate0_ref, shift_ref, scale_ref, gate1_ref, conv0_ref, h0_ref,
                 w_uv_ref, w_o_ref, g1_ref, w_in_ref, conv_w_ref, conv_b_ref, w_a_ref, b_a_ref,
                 w_x_ref, b_x_ref, lam_ref, w_ro_ref, gf_ref,
                 y_ref, conv_out_ref, h_out_ref,
                 xprev_scr, hs_scr, hcar_scr, *, nb, tt, o_is_value):
    j = pl.program_id(1)
    rows = nb * tt
    d = x_ref.shape[-1]
    w = lam_ref.shape[-1]
    bw = w // LRU_BLOCKS
    ng = tt // SUBLANES
    st = {}

    def layer0_out():
        zg = zg_ref[...].reshape(rows, d)
        if o_is_value:
            gated = o_ref[...].reshape(rows, d) * zg
        else:
            v = jnp.concatenate(
                [_dot(o_ref[:, hd].reshape(rows, KV_LORA).astype(BF16), w_uv_ref[hd])
                 for hd in range(N_HEADS)], axis=1)
            gated = (v * zg.astype(F32)).astype(BF16)
        out0 = _dot(gated, w_o_ref[...]).reshape(nb, tt, d)
        st["x1"] = x_ref[...] + gate0_ref[...] * out0

    def lru_in():
        h1 = _adaln(st["x1"], g1_ref[...], scale_ref[...], shift_ref[...])
        hb = h1.reshape(rows, d).astype(BF16)
        st["xb"] = _dot(hb, w_in_ref[:, 0:w])
        st["z"] = _dot(hb, w_in_ref[:, w:2 * w])

    def init_state():
        @pl.when(j == 0)
        def _():
            xprev_scr[:, 0:SUBLANES - (CONV_W - 1), :] = jnp.zeros((nb, SUBLANES - (CONV_W - 1), w), F32)
            xprev_scr[:, SUBLANES - (CONV_W - 1):SUBLANES, :] = conv0_ref[...]
            hcar_scr[...] = jnp.broadcast_to(h0_ref[...], hcar_scr.shape)

    def conv():
        xb3 = st.pop("xb").reshape(nb, tt, w)
        ext = jnp.concatenate([xprev_scr[...], xb3], axis=1).reshape(nb * (ng + 1), SUBLANES, w)
        xprev_scr[...] = xb3[:, tt - SUBLANES:tt, :]
        conv_out_ref[...] = xb3[:, tt - (CONV_W - 1):tt, :]
        gshape = (nb, ng, SUBLANES, w)
        gpos = lax.broadcasted_iota(jnp.int32, gshape, 2)
        xc = conv_b_ref[...] + xb3.reshape(gshape) * conv_w_ref[CONV_W - 1:CONV_W, :]
        for sh in range(1, CONV_W):
            rolled = pltpu.roll(ext, sh, 1).reshape(nb, ng + 1, SUBLANES, w)
            shifted = jnp.where(gpos >= sh, rolled[:, 1:], rolled[:, :ng])
            xc = xc + shifted * conv_w_ref[CONV_W - 1 - sh:CONV_W - sh, :]
        st["xc"] = xc.reshape(rows, w)

    def gates():
        xc2 = st["xc"]
        xcb = xc2.astype(BF16)
        st["t_r"] = jnp.tanh(jnp.concatenate(
            [_dot(xcb[:, n * bw:(n + 1) * bw], w_a_ref[n]) for n in range(LRU_BLOCKS)], axis=1) + b_a_ref[...])
        st["t_i"] = jnp.tanh(jnp.concatenate(
            [_dot(xcb[:, n * bw:(n + 1) * bw], w_x_ref[n]) for n in range(LRU_BLOCKS)], axis=1) + b_x_ref[...])

    def recurrence_inputs():
        neg_lam = -lam_ref[...]
        softplus = jnp.maximum(neg_lam, 0.0) + jnp.log1p(jnp.exp(-jnp.abs(neg_lam)))
        half_rate = (-0.5 * LRU_C) * softplus
        log_a = half_rate * st.pop("t_r") + half_rate
        a = jnp.exp(log_a)
        one_minus_a2 = -jnp.tanh(log_a) * (a * a + 1.0)
        st["a"] = a
        st["b"] = jnp.sqrt(one_minus_a2) * ((0.5 * st.pop("t_i") + 0.5) * st.pop("xc"))

    def scan_groups():
        a_g = st.pop("a").reshape(nb * ng, SUBLANES, w)
        b_g = st.pop("b").reshape(nb * ng, SUBLANES, w)
        tpos = lax.broadcasted_iota(jnp.int32, a_g.shape, 1)
        for sh in (1, 2, 4):
            keep = tpos >= sh
            a_prev = jnp.where(keep, pltpu.roll(a_g, sh, 1), 1.0)
            b_prev = jnp.where(keep, pltpu.roll(b_g, sh, 1), 0.0)
            b_g = a_g * b_prev + b_g
            a_g = a_g * a_prev
        st["a_g"] = a_g.reshape(nb, ng, SUBLANES, w)
        st["b_g"] = b_g.reshape(nb, ng, SUBLANES, w)

    def scan_carry():
        a_g = st.pop("a_g")
        b_g = st.pop("b_g")
        hprev = hcar_scr[...]
        for g in range(ng):
            hg = a_g[:, g] * hprev + b_g[:, g]
            hs_scr[:, g * SUBLANES:(g + 1) * SUBLANES, :] = hg
            hprev = jnp.broadcast_to(hg[:, SUBLANES - 1:SUBLANES, :], hg.shape)
        hcar_scr[...] = hprev
        h_out_ref[...] = hprev[:, 0:1, :]

    def lru_out():
        hs = hs_scr[...].reshape(rows, w)
        gated1 = (hs * _silu(st.pop("z"))).astype(BF16)
        out1 = _dot(gated1, w_ro_ref[...]).reshape(nb, tt, d)
        x2 = st.pop("x1") + gate1_ref[...] * out1
        y_ref[...] = _rms(x2, gf_ref[...])

    return [init_state, layer0_out, lru_in, conv, gates, recurrence_inputs, scan_groups, scan_carry, lru_out]


def _back_kernel(*refs, nb, tt, o_is_value):
    for phase in _back_phases(*refs, nb=nb, tt=tt, o_is_value=o_is_value):
        phase()


def _back(o, zg, x, mod0, mod1, conv0, h0, weights, *, nb, tt, o_is_value):
    bseq, t, d = x.shape
    w = conv0.shape[-1]
    if o_is_value:
        o_spec = pl.BlockSpec((nb, tt, d), lambda i, j: (i, j, 0))
    else:
        o_spec = pl.BlockSpec((nb, N_HEADS, tt, KV_LORA), lambda i, j: (i, 0, j, 0))
    in_specs = [
        o_spec,
        pl.BlockSpec((nb, tt, d), lambda i, j: (i, j, 0)),
        pl.BlockSpec((nb, tt, d), lambda i, j: (i, j, 0)),
        pl.BlockSpec((nb, 1, d), lambda i, j: (i, 0, 2)),
        pl.BlockSpec((nb, 1, d), lambda i, j: (i, 0, 0)),
        pl.BlockSpec((nb, 1, d), lambda i, j: (i, 0, 1)),
        pl.BlockSpec((nb, 1, d), lambda i, j: (i, 0, 2)),
        pl.BlockSpec((nb, CONV_W - 1, w), lambda i, j: (i, 0, 0)),
        pl.BlockSpec((nb, 1, w), lambda i, j: (i, 0, 0)),
    ] + [_full(wt.shape) for wt in weights]
    out_shape = (
        jax.ShapeDtypeStruct((bseq, t, d), F32),
        jax.ShapeDtypeStruct((bseq, CONV_W - 1, w), F32),
        jax.ShapeDtypeStruct((bseq, 1, w), F32),
    )
    out_specs = (
        pl.BlockSpec((nb, tt, d), lambda i, j: (i, j, 0)),
        pl.BlockSpec((nb, CONV_W - 1, w), lambda i, j: (i, 0, 0)),
        pl.BlockSpec((nb, 1, w), lambda i, j: (i, 0, 0)),
    )
    return pl.pallas_call(
        functools.partial(_back_kernel, nb=nb, tt=tt, o_is_value=o_is_value),
        out_shape=out_shape,
        grid=(bseq // nb, t // tt),
        in_specs=in_specs,
        out_specs=out_specs,
        scratch_shapes=[
            pltpu.VMEM((nb, SUBLANES, w), F32),
            pltpu.VMEM((nb, tt, w), F32),
            pltpu.VMEM((nb, SUBLANES, w), F32),
        ],
        compiler_params=_params("arbitrary", "arbitrary"),
        name="back_prompt" if o_is_value else "back_sample",
    )(o, zg, x, mod0, mod1, mod1, mod1, conv0, h0, *weights)


def _rope_tables(pos):
    half = QK_ROPE // 2
    inv = ROPE_THETA ** (-jnp.arange(half, dtype=F32) / half)
    ang = pos.astype(F32)[:, None] * inv[None, :]
    cos, sin = jnp.cos(ang), jnp.sin(ang)
    cos_t = jnp.tile(jnp.concatenate([cos, cos], axis=1), (1, N_HEADS))
    sin_t = jnp.tile(jnp.concatenate([-sin, sin], axis=1), (1, N_HEADS))
    return cos_t, sin_t


def _tiles(bseq, t):
    tt = min(ROW_TILE, t)
    nb = max(1, min(bseq, ROW_TILE // tt))
    assert t % tt == 0 and bseq % nb == 0 and tt % SUBLANES == 0
    return nb, tt


def kernel(x_prompt, x_sample, c_prompt, c_sample, cache_ckv, cache_kpe, page_table, state_conv, state_h,
           norm_g, w_ada, b_ada, final_g,
           a_w_in, a_g_q, a_w_qb, a_g_kv, a_w_uk, a_w_uv, a_w_o,
           r_w_in, r_conv_w, r_conv_b, r_w_a, r_b_a, r_w_x, r_b_x, r_lam, r_w_o):
    bp, tp, d = x_prompt.shape
    bs, ts, _ = x_sample.shape
    assert norm_g.shape[0] == 2 and a_w_in.shape[0] == 1 and r_w_in.shape[0] == 1
    w = r_lam.shape[-1]
    past_len = page_table.shape[1] * PAGE_SIZE

    w_in = a_w_in[0]
    c_kv, c_pe = Q_LORA + KV_LORA, Q_LORA + KV_LORA + QK_ROPE
    w_in_p = jnp.concatenate(
        [w_in[:, :c_kv], w_in[:, c_pe:], w_in[:, c_kv:c_pe], w_in[:, c_kv:c_pe]], axis=1).astype(BF16)
    w_qb = a_w_qb[0].reshape(Q_LORA, N_HEADS, QK_NOPE + QK_ROPE)
    w_qb_p = jnp.concatenate(
        [w_qb[:, :, :QK_NOPE].reshape(Q_LORA, N_HEADS * QK_NOPE),
         w_qb[:, :, QK_NOPE:].reshape(Q_LORA, N_HEADS * QK_ROPE)], axis=1).astype(BF16)
    w_uk_p = a_w_uk[0].transpose(1, 2, 0).astype(BF16)
    w_uv_p = a_w_uv[0].transpose(1, 0, 2).astype(BF16)
    row = lambda v: v.reshape(1, -1)
    front_w = (row(norm_g[0]), w_in_p, row(a_g_q[0]), w_qb_p, row(a_g_kv[0]), w_uk_p)
    back_w = (w_uv_p, a_w_o[0].astype(BF16), row(norm_g[1]), r_w_in[0].astype(BF16), r_conv_w[0],
              row(r_conv_b[0]), (0.5 * r_w_a[0]).astype(BF16), row(0.5 * r_b_a[0]),
              (0.5 * r_w_x[0]).astype(BF16), row(0.5 * r_b_x[0]),
              row(r_lam[0]), r_w_o[0].astype(BF16), row(final_g))

    mod = _modulation(jnp.concatenate([c_prompt, c_sample], axis=0), w_ada, b_ada)
    mod_p = [mod[i, :bp].reshape(bp, 1, 3 * d) for i in range(2)]
    mod_s = [mod[i, bp:].reshape(bs, 1, 3 * d) for i in range(2)]

    cos_p, sin_p = _rope_tables(jnp.arange(tp))
    cos_s, sin_s = _rope_tables(past_len + jnp.arange(ts))

    _, tq_p = _tiles(bp, tp)
    tt_p = min(PROMPT_ROW_TILE, tp)
    nb_s, tt_s = _tiles(bs, ts)

    qT_p, ckv_p, kpeT_p, ckvb_p, kpeb_p, ckvT_p, zg_p = _front_prompt(
        x_prompt, mod_p[0], cos_p, sin_p, front_w, tt=tt_p, tq=tq_p)
    qlat_s, qpe_s, ckv_s, kpe_s, zg_s = _front_sample(
        x_sample, mod_s[0], cos_s, sin_s, front_w, nb=nb_s, tt=tt_s)

    v_p = _attn_prompt(qT_p, ckvb_p, kpeb_p, ckvT_p, w_uv_p, tq=tq_p)

    o_s = _attn_sample(page_table, qlat_s, qpe_s, ckv_s, kpe_s, cache_ckv[0], jnp.swapaxes(cache_kpe[0], 1, 2))

    y_p, conv_p, h_p = _back(v_p, zg_p, x_prompt, mod_p[0], mod_p[1],
                             jnp.zeros((bp, CONV_W - 1, w), F32), jnp.zeros((bp, 1, w), F32),
                             back_w, nb=1, tt=tt_p, o_is_value=True)
    y_s, conv_s, h_s = _back(o_s, zg_s, x_sample, mod_s[0], mod_s[1],
                             state_conv[0], state_h[0].reshape(bs, 1, w),
                             back_w, nb=nb_s, tt=tt_s, o_is_value=False)

    return (y_p, y_s, ckv_p[None], jnp.swapaxes(kpeT_p, 1, 2)[None], ckv_s[None], kpe_s[None],
            conv_p[None], conv_s[None], h_p.reshape(1, bp, w), h_s.reshape(1, bs, w))
```

```python
import functools
import math

import jax
import jax.numpy as jnp
from jax import lax
from jax.experimental import pallas as pl
from jax.experimental.pallas import tpu as pltpu

F32 = jnp.float32
BF16 = jnp.bfloat16

N_HEADS = 8
QK_NOPE = 128
QK_ROPE = 64
V_DIM = 128
Q_LORA = 512
KV_LORA = 256
QK_CAT = KV_LORA + QK_ROPE
ROPE_THETA = 10000.0
SM_SCALE = 1.0 / math.sqrt(QK_NOPE + QK_ROPE)
Q_SCALE = SM_SCALE * math.log2(math.e)
NEG_INF = -1e30
LRU_BLOCKS = 4
CONV_W = 4
LRU_C = 8.0
NORM_EPS = 1e-6
PAGE_SIZE = 128

VMEM_LIMIT_BYTES = 56 * 1024 * 1024
SUBLANES = 8
ROW_TILE = 256
PROMPT_ROW_TILE = 512


def _dot(a, b):
    return jnp.dot(a, b, preferred_element_type=F32)


def _dot_nt(a, b):
    return lax.dot_general(a, b, (((1,), (1,)), ((), ())), preferred_element_type=F32)


def _rms(x, g):
    return x * lax.rsqrt(jnp.mean(x * x, axis=-1, keepdims=True) + NORM_EPS) * g


def _adaln(x, g, scale, shift):
    r = lax.rsqrt(jnp.mean(x * x, axis=-1, keepdims=True) + NORM_EPS)
    return x * r * (g * (1.0 + scale)) + shift


def _silu(x):
    half = 0.5 * x
    return half * jnp.tanh(half) + half


def _params(*sem):
    return pltpu.CompilerParams(dimension_semantics=sem, vmem_limit_bytes=VMEM_LIMIT_BYTES)


def _full(shape):
    return pl.BlockSpec(shape, lambda *_: (0,) * len(shape))


def _mod_kernel(c_ref, w_ref, b_ref, o_ref):
    o_ref[0] = _dot(c_ref[...].astype(BF16), w_ref[0].astype(BF16)) + b_ref[0]


def _modulation(c_all, w_ada, b_ada):
    depth, d, d3 = w_ada.shape
    n = c_all.shape[0]
    return pl.pallas_call(
        _mod_kernel,
        out_shape=jax.ShapeDtypeStruct((depth, n, d3), F32),
        grid=(depth, d3 // d),
        in_specs=[
            pl.BlockSpec((n, d), lambda i, j: (0, 0)),
            pl.BlockSpec((1, d, d), lambda i, j: (i, 0, j)),
            pl.BlockSpec((1, 1, d), lambda i, j: (i, 0, j)),
        ],
        out_specs=pl.BlockSpec((1, n, d), lambda i, j: (i, 0, j)),
        compiler_params=_params("arbitrary", "arbitrary"),
        name="mod",
    )(c_all, w_ada, b_ada.reshape(depth, 1, d3))


def _rope_rows(x, cos, sin_signed, nb, tt):
    width = x.shape[-1]
    lane = lax.broadcasted_iota(jnp.int32, x.shape, 1)
    first_half = (lane & (QK_ROPE - 1)) < (QK_ROPE // 2)
    partner = jnp.where(first_half,
                        pltpu.roll(x, width - QK_ROPE // 2, 1),
                        pltpu.roll(x, QK_ROPE // 2, 1))
    x3 = x.reshape(nb, tt, width)
    p3 = partner.reshape(nb, tt, width)
    return x3 * cos[None] + p3 * sin_signed[None]


def _front_compute(x_ref, shift_ref, scale_ref, cos_ref, sin_ref, g_ref, w_in_ref, gq_ref, w_qb_ref, gkv_ref,
                   nb, tt):
    rows = nb * tt
    d = x_ref.shape[-1]
    h = _adaln(x_ref[...], g_ref[...], scale_ref[...], shift_ref[...])
    hb = h.reshape(rows, d).astype(BF16)
    c0, c1, c2, c3 = Q_LORA, Q_LORA + KV_LORA, Q_LORA + KV_LORA + d, Q_LORA + KV_LORA + d + 2 * QK_ROPE
    u_q = _dot(hb, w_in_ref[:, 0:c0])
    u_kv = _dot(hb, w_in_ref[:, c0:c1])
    u_kpe = _dot(hb, w_in_ref[:, c2:c3])
    zg = lambda: _silu(_dot(hb, w_in_ref[:, c1:c2]))
    ckv = _rms(u_kv, gkv_ref[...])
    cos = cos_ref[...]
    sin = sin_ref[...]
    kr = _rope_rows(u_kpe, cos[:, 0:2 * QK_ROPE], sin[:, 0:2 * QK_ROPE], nb, tt)
    qn = _rms(u_q, gq_ref[...]).astype(BF16)
    q = _dot(qn, w_qb_ref[...])
    return zg, ckv, kr, q, cos, sin


def _front_prompt_kernel(x_ref, shift_ref, scale_ref, cos_ref, sin_ref, g_ref, w_in_ref, gq_ref, w_qb_ref,
                         gkv_ref, w_uk_ref,
                         qT_ref, ckv_ref, kpeT_ref, ckvb_ref, kpeb_ref, ckvT_ref, zg_ref, *, tt, tq):
    nope_w = N_HEADS * QK_NOPE
    pe_w = N_HEADS * QK_ROPE
    zg, ckv, kr, q, cos, sin = _front_compute(
        x_ref, shift_ref, scale_ref, cos_ref, sin_ref, g_ref, w_in_ref, gq_ref, w_qb_ref, gkv_ref, 1, tt)
    ckv_ref[0] = ckv
    ckvb_ref[0] = ckv.astype(ckvb_ref.dtype)
    ckvT = ckv.T.astype(ckvT_ref.dtype)
    kr2 = kr[0]
    kpeb_ref[0] = kr2[:, 0:QK_ROPE].astype(kpeb_ref.dtype)
    kpeT_ref[0] = kr2.T[0:QK_ROPE, :]
    tiles = [(s, slice(s * tq, (s + 1) * tq)) for s in range(tt // tq)]
    for s, tok in tiles:
        ckvT_ref[0, s] = ckvT[:, tok]
    qrT = (_rope_rows(q[:, nope_w:nope_w + pe_w], cos, sin, 1, tt)[0] * Q_SCALE).T
    for hd in range(N_HEADS):
        col = slice(hd * tq, (hd + 1) * tq)
        qh = q[:, hd * QK_NOPE:(hd + 1) * QK_NOPE].astype(BF16)
        qlT = (_dot(qh, w_uk_ref[hd]) * Q_SCALE).T
        for s, tok in tiles:
            qT_ref[0, s, 0:KV_LORA, col] = qlT[:, tok].astype(qT_ref.dtype)
            qT_ref[0, s, KV_LORA:QK_CAT, col] = qrT[hd * QK_ROPE:(hd + 1) * QK_ROPE, tok].astype(qT_ref.dtype)
    zg_ref[0] = zg().astype(zg_ref.dtype)


def _front_sample_kernel(x_ref, shift_ref, scale_ref, cos_ref, sin_ref, g_ref, w_in_ref, gq_ref, w_qb_ref,
                         gkv_ref, w_uk_ref,
                         qlat_ref, qpe_ref, ckv_ref, kpe_ref, zg_ref, *, nb, tt):
    d = x_ref.shape[-1]
    nope_w = N_HEADS * QK_NOPE
    pe_w = N_HEADS * QK_ROPE
    zg, ckv, kr, q, cos, sin = _front_compute(
        x_ref, shift_ref, scale_ref, cos_ref, sin_ref, g_ref, w_in_ref, gq_ref, w_qb_ref, gkv_ref, nb, tt)
    ckv_ref[...] = ckv.reshape(nb, tt, KV_LORA)
    kpe_ref[...] = kr[:, :, 0:QK_ROPE]
    for hd in range(N_HEADS):
        qh = q[:, hd * QK_NOPE:(hd + 1) * QK_NOPE].astype(BF16)
        ql = _dot(qh, w_uk_ref[hd]) * Q_SCALE
        qlat_ref[:, hd] = ql.reshape(nb, tt, KV_LORA)
    qr = _rope_rows(q[:, nope_w:nope_w + pe_w], cos, sin, nb, tt) * Q_SCALE
    for hd in range(N_HEADS):
        qpe_ref[:, hd] = qr[:, :, hd * QK_ROPE:(hd + 1) * QK_ROPE]
    zg_ref[...] = zg().reshape(nb, tt, d).astype(zg_ref.dtype)


def _front_in_specs(nb, tt, d, weights):
    pe_w = N_HEADS * QK_ROPE
    return [
        pl.BlockSpec((nb, tt, d), lambda i, j: (i, j, 0)),
        pl.BlockSpec((nb, 1, d), lambda i, j: (i, 0, 0)),
        pl.BlockSpec((nb, 1, d), lambda i, j: (i, 0, 1)),
        pl.BlockSpec((tt, pe_w), lambda i, j: (j, 0)),
        pl.BlockSpec((tt, pe_w), lambda i, j: (j, 0)),
    ] + [_full(w.shape) for w in weights]


def _front_prompt(x, mod3, cos_t, sin_t, weights, *, tt, tq):
    b, t, d = x.shape
    assert tt % tq == 0 and t % tt == 0
    nq = tt // tq
    nt = t // tq
    out_shape = (
        jax.ShapeDtypeStruct((b, nt, QK_CAT, N_HEADS * tq), BF16),
        jax.ShapeDtypeStruct((b, t, KV_LORA), F32),
        jax.ShapeDtypeStruct((b, QK_ROPE, t), F32),
        jax.ShapeDtypeStruct((b, t, KV_LORA), BF16),
        jax.ShapeDtypeStruct((b, t, QK_ROPE), BF16),
        jax.ShapeDtypeStruct((b, nt, KV_LORA, tq), BF16),
        jax.ShapeDtypeStruct((b, t, d), BF16),
    )
    out_specs = (
        pl.BlockSpec((1, nq, QK_CAT, N_HEADS * tq), lambda i, j: (i, j, 0, 0)),
        pl.BlockSpec((1, tt, KV_LORA), lambda i, j: (i, j, 0)),
        pl.BlockSpec((1, QK_ROPE, tt), lambda i, j: (i, 0, j)),
        pl.BlockSpec((1, tt, KV_LORA), lambda i, j: (i, j, 0)),
        pl.BlockSpec((1, tt, QK_ROPE), lambda i, j: (i, j, 0)),
        pl.BlockSpec((1, nq, KV_LORA, tq), lambda i, j: (i, j, 0, 0)),
        pl.BlockSpec((1, tt, d), lambda i, j: (i, j, 0)),
    )
    return pl.pallas_call(
        functools.partial(_front_prompt_kernel, tt=tt, tq=tq),
        out_shape=out_shape,
        grid=(b, t // tt),
        in_specs=_front_in_specs(1, tt, d, weights),
        out_specs=out_specs,
        compiler_params=_params("arbitrary", "arbitrary"),
        name="front_prompt",
    )(x, mod3, mod3, cos_t, sin_t, *weights)


def _front_sample(x, mod3, cos_t, sin_t, weights, *, nb, tt):
    bseq, t, d = x.shape
    out_shape = (
        jax.ShapeDtypeStruct((bseq, N_HEADS, t, KV_LORA), F32),
        jax.ShapeDtypeStruct((bseq, N_HEADS, t, QK_ROPE), F32),
        jax.ShapeDtypeStruct((bseq, t, KV_LORA), F32),
        jax.ShapeDtypeStruct((bseq, t, QK_ROPE), F32),
        jax.ShapeDtypeStruct((bseq, t, d), BF16),
    )
    out_specs = (
        pl.BlockSpec((nb, N_HEADS, tt, KV_LORA), lambda i, j: (i, 0, j, 0)),
        pl.BlockSpec((nb, N_HEADS, tt, QK_ROPE), lambda i, j: (i, 0, j, 0)),
        pl.BlockSpec((nb, tt, KV_LORA), lambda i, j: (i, j, 0)),
        pl.BlockSpec((nb, tt, QK_ROPE), lambda i, j: (i, j, 0)),
        pl.BlockSpec((nb, tt, d), lambda i, j: (i, j, 0)),
    )
    return pl.pallas_call(
        functools.partial(_front_sample_kernel, nb=nb, tt=tt),
        out_shape=out_shape,
        grid=(bseq // nb, t // tt),
        in_specs=_front_in_specs(nb, tt, d, weights),
        out_specs=out_specs,
        compiler_params=_params("arbitrary", "arbitrary"),
        name="front_sample",
    )(x, mod3, mod3, cos_t, sin_t, *weights)


def _attn_prompt_kernel(qT_ref, k_ref, kp_ref, vT_ref, w_uv_ref, o_ref, m_scr, l_scr, acc_scr, sa_scr, sb_scr,
                        *, tq):
    qi = pl.program_id(1)
    qlT = qT_ref[0, 0, 0:KV_LORA, :]
    qpT = qT_ref[0, 0, KV_LORA:QK_CAT, :]

    m_scr[...] = jnp.full(m_scr.shape, NEG_INF, F32)
    l_scr[...] = jnp.zeros(l_scr.shape, F32)
    acc_scr[...] = jnp.zeros(acc_scr.shape, F32)

    def scores(j):
        start = pl.multiple_of(j * tq, tq)
        k = k_ref[0, pl.ds(start, tq), :]
        kp = kp_ref[0, pl.ds(start, tq), :]
        return _dot(k, qlT) + _dot(kp, qpT)

    def accumulate(j, sT, masked):
        if masked:
            kpos = lax.broadcasted_iota(jnp.int32, sT.shape, 0)
            qpos = lax.broadcasted_iota(jnp.int32, sT.shape, 1) & (tq - 1)
            sT = jnp.where(kpos <= qpos, sT, NEG_INF)
        m_prev = m_scr[...]
        m_new = jnp.maximum(m_prev, jnp.max(sT, axis=0, keepdims=True))
        alpha = jnp.exp2(m_prev - m_new)
        pT = jnp.exp2(sT - m_new)
        l_scr[...] = alpha * l_scr[...] + jnp.sum(pT, axis=0, keepdims=True)
        acc_scr[...] = alpha * acc_scr[...] + _dot(vT_ref[0, j], pT.astype(BF16))
        m_scr[...] = m_new

    sa_scr[...] = scores(0)

    def body(i, carry):
        j = 2 * i
        sb_scr[...] = scores(j + 1)
        accumulate(j, sa_scr[...], False)
        sa_scr[...] = scores(j + 2)
        accumulate(j + 1, sb_scr[...], False)
        return carry

    lax.fori_loop(0, qi // 2, body, 0)

    @pl.when(qi % 2 == 0)
    def _():
        accumulate(qi, sa_scr[...], True)

    @pl.when(qi % 2 == 1)
    def _():
        sb_scr[...] = scores(qi)
        accumulate(qi - 1, sa_scr[...], False)
        accumulate(qi, sb_scr[...], True)

    oT = acc_scr[...] * (1.0 / l_scr[...])
    for hd in range(N_HEADS):
        o_h = oT[:, hd * tq:(hd + 1) * tq].T.astype(BF16)
        o_ref[0, :, hd * V_DIM:(hd + 1) * V_DIM] = _dot(o_h, w_uv_ref[hd]).astype(o_ref.dtype)


def _attn_prompt(qT, ckvb, kpeb, ckvT, w_uv_p, *, tq):
    b, nt, _, cols = qT.shape
    t = nt * tq
    assert tq & (tq - 1) == 0
    return pl.pallas_call(
        functools.partial(_attn_prompt_kernel, tq=tq),
        out_shape=jax.ShapeDtypeStruct((b, t, N_HEADS * V_DIM), BF16),
        grid=(b, nt),
        in_specs=[
            pl.BlockSpec((1, 1, QK_CAT, cols), lambda i, j: (i, j, 0, 0)),
            pl.BlockSpec((1, t, KV_LORA), lambda i, j: (i, 0, 0)),
            pl.BlockSpec((1, t, QK_ROPE), lambda i, j: (i, 0, 0)),
            pl.BlockSpec((1, nt, KV_LORA, tq), lambda i, j: (i, 0, 0, 0)),
            _full(w_uv_p.shape),
        ],
        out_specs=pl.BlockSpec((1, tq, N_HEADS * V_DIM), lambda i, j: (i, j, 0)),
        scratch_shapes=[
            pltpu.VMEM((1, cols), F32),
            pltpu.VMEM((1, cols), F32),
            pltpu.VMEM((KV_LORA, cols), F32),
            pltpu.VMEM((tq, cols), F32),
            pltpu.VMEM((tq, cols), F32),
        ],
        compiler_params=_params("arbitrary", "arbitrary"),
        name="attn_prompt",
    )(qT, ckvb, kpeb, ckvT, w_uv_p)


SAMPLE_CHUNK_PAGES = 64
SAMPLE_SLOTS = 3


def _sample_stages(pt_ref, ql_ref, qp_ref, kn_ref, kpn_ref, cache_k, cache_pT, o_ref,
                   kbuf, pbuf, kb_scr, sems, *, step, n_steps, spp, n_chunks, cp, ts):
    rows = N_HEADS * ts
    nseq = n_steps * spp
    ahead = SAMPLE_SLOTS - 1

    def page_copies(seq, chunk, slot, page):
        phys = pt_ref[seq, chunk * cp + page]
        dst = pl.ds(page * PAGE_SIZE, PAGE_SIZE)
        return (
            pltpu.make_async_copy(cache_k.at[phys], kbuf.at[slot, dst], sems.at[slot, 0]),
            pltpu.make_async_copy(cache_pT.at[phys], pbuf.at[slot, :, dst], sems.at[slot, 1]),
        )

    def start_chunk(seq, chunk, slot):
        for page in range(cp):
            for c in page_copies(seq, chunk, slot, page):
                c.start()

    def wait_chunk(seq, chunk, slot):
        for page in range(cp):
            for c in page_copies(seq, chunk, slot, page):
                c.wait()

    def update(st, s, v):
        m_new = jnp.maximum(st["m"], jnp.max(s, axis=1, keepdims=True))
        alpha = jnp.exp2(st["m"] - m_new)
        p = jnp.exp2(s - m_new)
        st["l"] = alpha * st["l"] + jnp.sum(p, axis=1, keepdims=True)
        st["acc"] = alpha * st["acc"] + _dot(p.astype(BF16), v)
        st["m"] = m_new

    def make_stage(q, c, st):
        seq = step * spp + q

        def stage():
            if q == 0 and c == 0:
                @pl.when(step == 0)
                def _():
                    for c0 in range(ahead):
                        start_chunk(0, c0, c0 % SAMPLE_SLOTS)
            if c == 0:
                st["ql"] = ql_ref[q].reshape(rows, KV_LORA).astype(BF16)
                st["qp"] = qp_ref[q].reshape(rows, QK_ROPE).astype(BF16)
                st["m"] = jnp.full((rows, 1), NEG_INF, F32)
                st["l"] = jnp.zeros((rows, 1), F32)
                st["acc"] = jnp.zeros((rows, KV_LORA), F32)
            if c < n_chunks:
                slot = lax.rem(seq * n_chunks + c, SAMPLE_SLOTS)
                nxt = c + ahead
                nxt_slot = lax.rem(seq * n_chunks + nxt, SAMPLE_SLOTS)
                if nxt < n_chunks:
                    start_chunk(seq, nxt, nxt_slot)
                else:
                    @pl.when(seq + 1 < nseq)
                    def _():
                        start_chunk(seq + 1, nxt - n_chunks, nxt_slot)
                wait_chunk(seq, c, slot)
                k = kbuf[slot].astype(BF16)
                kb_scr[c % 2] = k
                s_cur = _dot_nt(st["ql"], k) + _dot(st["qp"], pbuf[slot].astype(BF16))
            if c == 0:
                kn = kn_ref[q].astype(BF16)
                kpn = kpn_ref[q].astype(BF16)
                s = _dot_nt(st["ql"], kn) + _dot_nt(st["qp"], kpn)
                s3 = s.reshape(N_HEADS, ts, ts)
                qpos = lax.broadcasted_iota(jnp.int32, s3.shape, 1)
                kpos = lax.broadcasted_iota(jnp.int32, s3.shape, 2)
                update(st, jnp.where(kpos <= qpos, s3, NEG_INF).reshape(rows, ts), kn)
            else:
                update(st, st["s_prev"], kb_scr[(c - 1) % 2])
            if c < n_chunks:
                st["s_prev"] = s_cur
            else:
                o_ref[q] = (st["acc"] / st["l"]).reshape(N_HEADS, ts, KV_LORA)

        return stage

    stages = []
    for q in range(spp):
        st = {}
        stages += [make_stage(q, c, st) for c in range(n_chunks + 1)]
    return stages


def _attn_sample_kernel(pt_ref, ql_ref, qp_ref, kn_ref, kpn_ref, cache_k, cache_pT, o_ref,
                        kbuf, pbuf, kb_scr, sems, *, n_chunks, cp, ts):
    for stage in _sample_stages(pt_ref, ql_ref, qp_ref, kn_ref, kpn_ref, cache_k, cache_pT, o_ref,
                                kbuf, pbuf, kb_scr, sems, step=pl.program_id(0), n_steps=pl.num_programs(0),
                                spp=1, n_chunks=n_chunks, cp=cp, ts=ts):
        stage()


def _attn_sample(page_table, qlat, qpe, ckv_new, kpe_new, cache_ckv, cache_kpeT):
    bs, _, ts, _ = qlat.shape
    n_pages = page_table.shape[1]
    cp = min(SAMPLE_CHUNK_PAGES, n_pages // 2)
    assert n_pages % cp == 0
    n_chunks = n_pages // cp
    assert n_chunks >= SAMPLE_SLOTS - 1
    grid_spec = pltpu.PrefetchScalarGridSpec(
        num_scalar_prefetch=1,
        grid=(bs,),
        in_specs=[
            pl.BlockSpec((1, N_HEADS, ts, KV_LORA), lambda i, pt: (i, 0, 0, 0)),
            pl.BlockSpec((1, N_HEADS, ts, QK_ROPE), lambda i, pt: (i, 0, 0, 0)),
            pl.BlockSpec((1, ts, KV_LORA), lambda i, pt: (i, 0, 0)),
            pl.BlockSpec((1, ts, QK_ROPE), lambda i, pt: (i, 0, 0)),
            pl.BlockSpec(memory_space=pl.ANY),
            pl.BlockSpec(memory_space=pl.ANY),
        ],
        out_specs=pl.BlockSpec((1, N_HEADS, ts, KV_LORA), lambda i, pt: (i, 0, 0, 0)),
        scratch_shapes=[
            pltpu.VMEM((SAMPLE_SLOTS, cp * PAGE_SIZE, KV_LORA), F32),
            pltpu.VMEM((SAMPLE_SLOTS, QK_ROPE, cp * PAGE_SIZE), F32),
            pltpu.VMEM((2, cp * PAGE_SIZE, KV_LORA), BF16),
            pltpu.SemaphoreType.DMA((SAMPLE_SLOTS, 2)),
        ],
    )
    return pl.pallas_call(
        functools.partial(_attn_sample_kernel, n_chunks=n_chunks, cp=cp, ts=ts),
        out_shape=jax.ShapeDtypeStruct((bs, N_HEADS, ts, KV_LORA), F32),
        grid_spec=grid_spec,
        compiler_params=_params("arbitrary"),
        name="attn_sample",
    )(page_table, qlat, qpe, ckv_new, kpe_new, cache_ckv, cache_kpeT)


def _back_phases(o_ref, zg_ref, x_ref, gate0_ref, shift_ref, scale_ref, gate1_ref, conv0_ref, h0_ref,
                 w_uv_ref, w_o_ref, g1_ref, w_in_ref, conv_w_ref, conv_b_ref, w_a_ref, b_a_ref,
                 w_x_ref, b_x_ref, lam_ref, w_ro_ref, gf_ref,
                 y_ref, conv_out_ref, h_out_ref,
                 xprev_scr, hs_scr, hcar_scr, *, nb, tt, o_is_value):
    j = pl.program_id(1)
    rows = nb * tt
    d = x_ref.shape[-1]
    w = lam_ref.shape[-1]
    bw = w // LRU_BLOCKS
    ng = tt // SUBLANES
    st = {}

    def layer0_out():
        zg = zg_ref[...].reshape(rows, d)
        if o_is_value:
            gated = o_ref[...].reshape(rows, d) * zg
        else:
            v = jnp.concatenate(
                [_dot(o_ref[:, hd].reshape(rows, KV_LORA).astype(BF16), w_uv_ref[hd])
                 for hd in range(N_HEADS)], axis=1)
            gated = (v * zg.astype(F32)).astype(BF16)
        out0 = _dot(gated, w_o_ref[...]).reshape(nb, tt, d)
        st["x1"] = x_ref[...] + gate0_ref[...] * out0

    def lru_in():
        h1 = _adaln(st["x1"], g1_ref[...], scale_ref[...], shift_ref[...])
        hb = h1.reshape(rows, d).astype(BF16)
        st["xb"] = _dot(hb, w_in_ref[:, 0:w])
        st["hb"] = hb

    def gate_branch():
        st["zg"] = _silu(_dot(st.pop("hb"), w_in_ref[:, w:2 * w]))

    def init_state():
        @pl.when(j == 0)
        def _():
            xprev_scr[:, 0:SUBLANES - (CONV_W - 1), :] = jnp.zeros((nb, SUBLANES - (CONV_W - 1), w), F32)
            xprev_scr[:, SUBLANES - (CONV_W - 1):SUBLANES, :] = conv0_ref[...]
            hcar_scr[...] = jnp.broadcast_to(h0_ref[...], hcar_scr.shape)

    def conv():
        xb3 = st.pop("xb").reshape(nb, tt, w)
        ext = jnp.concatenate([xprev_scr[...], xb3], axis=1).reshape(nb * (ng + 1), SUBLANES, w)
        xprev_scr[...] = xb3[:, tt - SUBLANES:tt, :]
        conv_out_ref[...] = xb3[:, tt - (CONV_W - 1):tt, :]
        gshape = (nb, ng, SUBLANES, w)
        gpos = lax.broadcasted_iota(jnp.int32, gshape, 2)
        xc = conv_b_ref[...] + xb3.reshape(gshape) * conv_w_ref[CONV_W - 1:CONV_W, :]
        for sh in range(1, CONV_W):
            rolled = pltpu.roll(ext, sh, 1).reshape(nb, ng + 1, SUBLANES, w)
            shifted = jnp.where(gpos >= sh, rolled[:, 1:], rolled[:, :ng])
            xc = xc + shifted * conv_w_ref[CONV_W - 1 - sh:CONV_W - sh, :]
        st["xc"] = xc.reshape(rows, w)

    def gates():
        xc2 = st["xc"]
        xcb = xc2.astype(BF16)
        st["t_r"] = jnp.tanh(jnp.concatenate(
            [_dot(xcb[:, n * bw:(n + 1) * bw], w_a_ref[n]) for n in range(LRU_BLOCKS)], axis=1) + b_a_ref[...])
        st["t_i"] = jnp.tanh(jnp.concatenate(
            [_dot(xcb[:, n * bw:(n + 1) * bw], w_x_ref[n]) for n in range(LRU_BLOCKS)], axis=1) + b_x_ref[...])

    def recurrence_inputs():
        neg_lam = -lam_ref[...]
        softplus = jnp.maximum(neg_lam, 0.0) + jnp.log1p(jnp.exp(-jnp.abs(neg_lam)))
        half_rate = (-0.5 * LRU_C) * softplus
        log_a = half_rate * st.pop("t_r") + half_rate
        a = jnp.exp(log_a)
        one_minus_a2 = -jnp.tanh(log_a) * (a * a + 1.0)
        st["a"] = a
        st["b"] = jnp.sqrt(one_minus_a2) * ((0.5 * st.pop("t_i") + 0.5) * st.pop("xc"))

    def scan_groups():
        a_g = st.pop("a").reshape(nb * ng, SUBLANES, w)
        b_g = st.pop("b").reshape(nb * ng, SUBLANES, w)
        tpos = lax.broadcasted_iota(jnp.int32, a_g.shape, 1)
        for sh in (1, 2, 4):
            keep = tpos >= sh
            a_prev = jnp.where(keep, pltpu.roll(a_g, sh, 1), 1.0)
            b_prev = jnp.where(keep, pltpu.roll(b_g, sh, 1), 0.0)
            b_g = a_g * b_prev + b_g
            a_g = a_g * a_prev
        st["a_g"] = a_g.reshape(nb, ng, SUBLANES, w)
        st["b_g"] = b_g.reshape(nb, ng, SUBLANES, w)

    def scan_carry():
        a_g = st.pop("a_g")
        b_g = st.pop("b_g")
        hprev = hcar_scr[...]
        for g in range(ng):
            hg = a_g[:, g] * hprev + b_g[:, g]
            hs_scr[:, g * SUBLANES:(g + 1) * SUBLANES, :] = hg
            hprev = jnp.broadcast_to(hg[:, SUBLANES - 1:SUBLANES, :], hg.shape)
        hcar_scr[...] = hprev
        h_out_ref[...] = hprev[:, 0:1, :]

    def lru_out():
        hs = hs_scr[...].reshape(rows, w)
        gated1 = (hs * st.pop("zg")).astype(BF16)
        out1 = _dot(gated1, w_ro_ref[...]).reshape(nb, tt, d)
        x2 = st.pop("x1") + gate1_ref[...] * out1
        y_ref[...] = _rms(x2, gf_ref[...])

    return [init_state, layer0_out, lru_in, conv, gates, recurrence_inputs, gate_branch, scan_groups, scan_carry,
            lru_out]


def _back_kernel(*refs, nb, tt, o_is_value):
    for phase in _back_phases(*refs, nb=nb, tt=tt, o_is_value=o_is_value):
        phase()


def _back(o, zg, x, mod0, mod1, conv0, h0, weights, *, nb, tt, o_is_value):
    bseq, t, d = x.shape
    w = conv0.shape[-1]
    if o_is_value:
        o_spec = pl.BlockSpec((nb, tt, d), lambda i, j: (i, j, 0))
    else:
        o_spec = pl.BlockSpec((nb, N_HEADS, tt, KV_LORA), lambda i, j: (i, 0, j, 0))
    in_specs = [
        o_spec,
        pl.BlockSpec((nb, tt, d), lambda i, j: (i, j, 0)),
        pl.BlockSpec((nb, tt, d), lambda i, j: (i, j, 0)),
        pl.BlockSpec((nb, 1, d), lambda i, j: (i, 0, 2)),
        pl.BlockSpec((nb, 1, d), lambda i, j: (i, 0, 0)),
        pl.BlockSpec((nb, 1, d), lambda i, j: (i, 0, 1)),
        pl.BlockSpec((nb, 1, d), lambda i, j: (i, 0, 2)),
        pl.BlockSpec((nb, CONV_W - 1, w), lambda i, j: (i, 0, 0)),
        pl.BlockSpec((nb, 1, w), lambda i, j: (i, 0, 0)),
    ] + [_full(wt.shape) for wt in weights]
    out_shape = (
        jax.ShapeDtypeStruct((bseq, t, d), F32),
        jax.ShapeDtypeStruct((bseq, CONV_W - 1, w), F32),
        jax.ShapeDtypeStruct((bseq, 1, w), F32),
    )
    out_specs = (
        pl.BlockSpec((nb, tt, d), lambda i, j: (i, j, 0)),
        pl.BlockSpec((nb, CONV_W - 1, w), lambda i, j: (i, 0, 0)),
        pl.BlockSpec((nb, 1, w), lambda i, j: (i, 0, 0)),
    )
    return pl.pallas_call(
        functools.partial(_back_kernel, nb=nb, tt=tt, o_is_value=o_is_value),
        out_shape=out_shape,
        grid=(bseq // nb, t // tt),
        in_specs=in_specs,
        out_specs=out_specs,
        scratch_shapes=[
            pltpu.VMEM((nb, SUBLANES, w), F32),
            pltpu.VMEM((nb, tt, w), F32),
            pltpu.VMEM((nb, SUBLANES, w), F32),
        ],
        compiler_params=_params("arbitrary", "arbitrary"),
        name="back_prompt" if o_is_value else "back_sample",
    )(o, zg, x, mod0, mod1, mod1, mod1, conv0, h0, *weights)


def _rope_tables(pos):
    half = QK_ROPE // 2
    inv = ROPE_THETA ** (-jnp.arange(half, dtype=F32) / half)
    ang = pos.astype(F32)[:, None] * inv[None, :]
    cos, sin = jnp.cos(ang), jnp.sin(ang)
    cos_t = jnp.tile(jnp.concatenate([cos, cos], axis=1), (1, N_HEADS))
    sin_t = jnp.tile(jnp.concatenate([-sin, sin], axis=1), (1, N_HEADS))
    return cos_t, sin_t


def _tiles(bseq, t):
    tt = min(ROW_TILE, t)
    nb = max(1, min(bseq, ROW_TILE // tt))
    assert t % tt == 0 and bseq % nb == 0 and tt % SUBLANES == 0
    return nb, tt


def kernel(x_prompt, x_sample, c_prompt, c_sample, cache_ckv, cache_kpe, page_table, state_conv, state_h,
           norm_g, w_ada, b_ada, final_g,
           a_w_in, a_g_q, a_w_qb, a_g_kv, a_w_uk, a_w_uv, a_w_o,
           r_w_in, r_conv_w, r_conv_b, r_w_a, r_b_a, r_w_x, r_b_x, r_lam, r_w_o):
    bp, tp, d = x_prompt.shape
    bs, ts, _ = x_sample.shape
    assert norm_g.shape[0] == 2 and a_w_in.shape[0] == 1 and r_w_in.shape[0] == 1
    w = r_lam.shape[-1]
    past_len = page_table.shape[1] * PAGE_SIZE

    w_in = a_w_in[0]
    c_kv, c_pe = Q_LORA + KV_LORA, Q_LORA + KV_LORA + QK_ROPE
    w_in_p = jnp.concatenate(
        [w_in[:, :c_kv], w_in[:, c_pe:], w_in[:, c_kv:c_pe], w_in[:, c_kv:c_pe]], axis=1).astype(BF16)
    w_qb = a_w_qb[0].reshape(Q_LORA, N_HEADS, QK_NOPE + QK_ROPE)
    w_qb_p = jnp.concatenate(
        [w_qb[:, :, :QK_NOPE].reshape(Q_LORA, N_HEADS * QK_NOPE),
         w_qb[:, :, QK_NOPE:].reshape(Q_LORA, N_HEADS * QK_ROPE)], axis=1).astype(BF16)
    w_uk_p = a_w_uk[0].transpose(1, 2, 0).astype(BF16)
    w_uv_p = a_w_uv[0].transpose(1, 0, 2).astype(BF16)
    row = lambda v: v.reshape(1, -1)
    front_w = (row(norm_g[0]), w_in_p, row(a_g_q[0]), w_qb_p, row(a_g_kv[0]), w_uk_p)
    back_w = (w_uv_p, a_w_o[0].astype(BF16), row(norm_g[1]), r_w_in[0].astype(BF16), r_conv_w[0],
              row(r_conv_b[0]), (0.5 * r_w_a[0]).astype(BF16), row(0.5 * r_b_a[0]),
              (0.5 * r_w_x[0]).astype(BF16), row(0.5 * r_b_x[0]),
              row(r_lam[0]), r_w_o[0].astype(BF16), row(final_g))

    mod = _modulation(jnp.concatenate([c_prompt, c_sample], axis=0), w_ada, b_ada)
    mod_p = [mod[i, :bp].reshape(bp, 1, 3 * d) for i in range(2)]
    mod_s = [mod[i, bp:].reshape(bs, 1, 3 * d) for i in range(2)]

    cos_p, sin_p = _rope_tables(jnp.arange(tp))
    cos_s, sin_s = _rope_tables(past_len + jnp.arange(ts))

    _, tq_p = _tiles(bp, tp)
    tt_p = min(PROMPT_ROW_TILE, tp)
    nb_s, tt_s = _tiles(bs, ts)

    qT_p, ckv_p, kpeT_p, ckvb_p, kpeb_p, ckvT_p, zg_p = _front_prompt(
        x_prompt, mod_p[0], cos_p, sin_p, front_w, tt=tt_p, tq=tq_p)
    qlat_s, qpe_s, ckv_s, kpe_s, zg_s = _front_sample(
        x_sample, mod_s[0], cos_s, sin_s, front_w, nb=nb_s, tt=tt_s)

    v_p = _attn_prompt(qT_p, ckvb_p, kpeb_p, ckvT_p, w_uv_p, tq=tq_p)

    o_s = _attn_sample(page_table, qlat_s, qpe_s, ckv_s, kpe_s, cache_ckv[0], jnp.swapaxes(cache_kpe[0], 1, 2))

    y_p, conv_p, h_p = _back(v_p, zg_p, x_prompt, mod_p[0], mod_p[1],
                             jnp.zeros((bp, CONV_W - 1, w), F32), jnp.zeros((bp, 1, w), F32),
                             back_w, nb=1, tt=tt_p, o_is_value=True)
    y_s, conv_s, h_s = _back(o_s, zg_s, x_sample, mod_s[0], mod_s[1],
                             state_conv[0], state_h[0].reshape(bs, 1, w),
                             back_w, nb=nb_s, tt=tt_s, o_is_value=False)

    return (y_p, y_s, ckv_p[None], jnp.swapaxes(kpeT_p, 1, 2)[None], ckv_s[None], kpe_s[None],
            conv_p[None], conv_s[None], h_p.reshape(1, bp, w), h_s.reshape(1, bs, w))
```

```python
import functools
import math

import jax
import jax.numpy as jnp
from jax import lax
from jax.experimental import pallas as pl
from jax.experimental.pallas import tpu as pltpu

F32 = jnp.float32
BF16 = jnp.bfloat16

N_HEADS = 8
QK_NOPE = 128
QK_ROPE = 64
V_DIM = 128
Q_LORA = 512
KV_LORA = 256
QK_CAT = KV_LORA + QK_ROPE
ROPE_THETA = 10000.0
SM_SCALE = 1.0 / math.sqrt(QK_NOPE + QK_ROPE)
Q_SCALE = SM_SCALE * math.log2(math.e)
NEG_INF = -1e30
LRU_BLOCKS = 4
CONV_W = 4
LRU_C = 8.0
NORM_EPS = 1e-6
PAGE_SIZE = 128

VMEM_LIMIT_BYTES = 56 * 1024 * 1024
SUBLANES = 8
ROW_TILE = 256
PROMPT_ROW_TILE = 512


def _dot(a, b):
    return jnp.dot(a, b, preferred_element_type=F32)


def _dot_nt(a, b):
    return lax.dot_general(a, b, (((1,), (1,)), ((), ())), preferred_element_type=F32)


def _rms(x, g):
    return x * lax.rsqrt(jnp.mean(x * x, axis=-1, keepdims=True) + NORM_EPS) * g


def _adaln(x, g, scale, shift):
    r = lax.rsqrt(jnp.mean(x * x, axis=-1, keepdims=True) + NORM_EPS)
    return x * r * (g * (1.0 + scale)) + shift


def _silu(x):
    half = 0.5 * x
    return half * jnp.tanh(half) + half


def _params(*sem):
    return pltpu.CompilerParams(dimension_semantics=sem, vmem_limit_bytes=VMEM_LIMIT_BYTES)


def _full(shape):
    return pl.BlockSpec(shape, lambda *_: (0,) * len(shape))


def _mod_kernel(c_ref, w_ref, b_ref, o_ref):
    o_ref[0] = _dot(c_ref[...].astype(BF16), w_ref[0].astype(BF16)) + b_ref[0]


def _modulation(c_all, w_ada, b_ada):
    depth, d, d3 = w_ada.shape
    n = c_all.shape[0]
    return pl.pallas_call(
        _mod_kernel,
        out_shape=jax.ShapeDtypeStruct((depth, n, d3), F32),
        grid=(depth, d3 // d),
        in_specs=[
            pl.BlockSpec((n, d), lambda i, j: (0, 0)),
            pl.BlockSpec((1, d, d), lambda i, j: (i, 0, j)),
            pl.BlockSpec((1, 1, d), lambda i, j: (i, 0, j)),
        ],
        out_specs=pl.BlockSpec((1, n, d), lambda i, j: (i, 0, j)),
        compiler_params=_params("arbitrary", "arbitrary"),
        name="mod",
    )(c_all, w_ada, b_ada.reshape(depth, 1, d3))


def _rope_rows(x, cos, sin_signed, nb, tt):
    width = x.shape[-1]
    lane = lax.broadcasted_iota(jnp.int32, x.shape, 1)
    first_half = (lane & (QK_ROPE - 1)) < (QK_ROPE // 2)
    partner = jnp.where(first_half,
                        pltpu.roll(x, width - QK_ROPE // 2, 1),
                        pltpu.roll(x, QK_ROPE // 2, 1))
    x3 = x.reshape(nb, tt, width)
    p3 = partner.reshape(nb, tt, width)
    return x3 * cos[None] + p3 * sin_signed[None]


def _front_compute(x_ref, shift_ref, scale_ref, cos_ref, sin_ref, g_ref, w_in_ref, gq_ref, w_qb_ref, gkv_ref,
                   nb, tt):
    rows = nb * tt
    d = x_ref.shape[-1]
    h = _adaln(x_ref[...], g_ref[...], scale_ref[...], shift_ref[...])
    hb = h.reshape(rows, d).astype(BF16)
    c0, c1, c2, c3 = Q_LORA, Q_LORA + KV_LORA, Q_LORA + KV_LORA + d, Q_LORA + KV_LORA + d + 2 * QK_ROPE
    u_q = _dot(hb, w_in_ref[:, 0:c0])
    u_kv = _dot(hb, w_in_ref[:, c0:c1])
    u_kpe = _dot(hb, w_in_ref[:, c2:c3])
    zg = lambda: _silu(_dot(hb, w_in_ref[:, c1:c2]))
    ckv = _rms(u_kv, gkv_ref[...])
    cos = cos_ref[...]
    sin = sin_ref[...]
    kr = _rope_rows(u_kpe, cos[:, 0:2 * QK_ROPE], sin[:, 0:2 * QK_ROPE], nb, tt)
    qn = _rms(u_q, gq_ref[...]).astype(BF16)
    q = _dot(qn, w_qb_ref[...])
    return zg, ckv, kr, q, cos, sin


def _front_prompt_kernel(x_ref, shift_ref, scale_ref, cos_ref, sin_ref, g_ref, w_in_ref, gq_ref, w_qb_ref,
                         gkv_ref, w_uk_ref,
                         qT_ref, ckv_ref, kpeT_ref, ckvb_ref, kpeb_ref, ckvT_ref, zg_ref, *, tt, tq):
    nope_w = N_HEADS * QK_NOPE
    pe_w = N_HEADS * QK_ROPE
    zg, ckv, kr, q, cos, sin = _front_compute(
        x_ref, shift_ref, scale_ref, cos_ref, sin_ref, g_ref, w_in_ref, gq_ref, w_qb_ref, gkv_ref, 1, tt)
    ckv_ref[0] = ckv
    ckvb_ref[0] = ckv.astype(ckvb_ref.dtype)
    ckvT = ckv.T.astype(ckvT_ref.dtype)
    kr2 = kr[0]
    kpeb_ref[0] = kr2[:, 0:QK_ROPE].astype(kpeb_ref.dtype)
    kpeT_ref[0] = kr2.T[0:QK_ROPE, :]
    tiles = [(s, slice(s * tq, (s + 1) * tq)) for s in range(tt // tq)]
    for s, tok in tiles:
        ckvT_ref[0, s] = ckvT[:, tok]
    qrT = (_rope_rows(q[:, nope_w:nope_w + pe_w], cos, sin, 1, tt)[0] * Q_SCALE).T
    for hd in range(N_HEADS):
        col = slice(hd * tq, (hd + 1) * tq)
        qh = q[:, hd * QK_NOPE:(hd + 1) * QK_NOPE].astype(BF16)
        qlT = (_dot(qh, w_uk_ref[hd]) * Q_SCALE).T
        for s, tok in tiles:
            qT_ref[0, s, 0:KV_LORA, col] = qlT[:, tok].astype(qT_ref.dtype)
            qT_ref[0, s, KV_LORA:QK_CAT, col] = qrT[hd * QK_ROPE:(hd + 1) * QK_ROPE, tok].astype(qT_ref.dtype)
    zg_ref[0] = zg().astype(zg_ref.dtype)


def _front_sample_kernel(x_ref, shift_ref, scale_ref, cos_ref, sin_ref, g_ref, w_in_ref, gq_ref, w_qb_ref,
                         gkv_ref, w_uk_ref,
                         qlat_ref, qpe_ref, ckv_ref, kpe_ref, zg_ref, *, nb, tt):
    d = x_ref.shape[-1]
    nope_w = N_HEADS * QK_NOPE
    pe_w = N_HEADS * QK_ROPE
    zg, ckv, kr, q, cos, sin = _front_compute(
        x_ref, shift_ref, scale_ref, cos_ref, sin_ref, g_ref, w_in_ref, gq_ref, w_qb_ref, gkv_ref, nb, tt)
    ckv_ref[...] = ckv.reshape(nb, tt, KV_LORA)
    kpe_ref[...] = kr[:, :, 0:QK_ROPE]
    for hd in range(N_HEADS):
        qh = q[:, hd * QK_NOPE:(hd + 1) * QK_NOPE].astype(BF16)
        ql = _dot(qh, w_uk_ref[hd]) * Q_SCALE
        qlat_ref[:, hd] = ql.reshape(nb, tt, KV_LORA)
    qr = _rope_rows(q[:, nope_w:nope_w + pe_w], cos, sin, nb, tt) * Q_SCALE
    for hd in range(N_HEADS):
        qpe_ref[:, hd] = qr[:, :, hd * QK_ROPE:(hd + 1) * QK_ROPE]
    zg_ref[...] = zg().reshape(nb, tt, d).astype(zg_ref.dtype)


def _front_in_specs(nb, tt, d, weights):
    pe_w = N_HEADS * QK_ROPE
    return [
        pl.BlockSpec((nb, tt, d), lambda i, j: (i, j, 0)),
        pl.BlockSpec((nb, 1, d), lambda i, j: (i, 0, 0)),
        pl.BlockSpec((nb, 1, d), lambda i, j: (i, 0, 1)),
        pl.BlockSpec((tt, pe_w), lambda i, j: (j, 0)),
        pl.BlockSpec((tt, pe_w), lambda i, j: (j, 0)),
    ] + [_full(w.shape) for w in weights]


def _front_prompt(x, mod3, cos_t, sin_t, weights, *, tt, tq):
    b, t, d = x.shape
    assert tt % tq == 0 and t % tt == 0
    nq = tt // tq
    nt = t // tq
    out_shape = (
        jax.ShapeDtypeStruct((b, nt, QK_CAT, N_HEADS * tq), BF16),
        jax.ShapeDtypeStruct((b, t, KV_LORA), F32),
        jax.ShapeDtypeStruct((b, QK_ROPE, t), F32),
        jax.ShapeDtypeStruct((b, t, KV_LORA), BF16),
        jax.ShapeDtypeStruct((b, t, QK_ROPE), BF16),
        jax.ShapeDtypeStruct((b, nt, KV_LORA, tq), BF16),
        jax.ShapeDtypeStruct((b, t, d), BF16),
    )
    out_specs = (
        pl.BlockSpec((1, nq, QK_CAT, N_HEADS * tq), lambda i, j: (i, j, 0, 0)),
        pl.BlockSpec((1, tt, KV_LORA), lambda i, j: (i, j, 0)),
        pl.BlockSpec((1, QK_ROPE, tt), lambda i, j: (i, 0, j)),
        pl.BlockSpec((1, tt, KV_LORA), lambda i, j: (i, j, 0)),
        pl.BlockSpec((1, tt, QK_ROPE), lambda i, j: (i, j, 0)),
        pl.BlockSpec((1, nq, KV_LORA, tq), lambda i, j: (i, j, 0, 0)),
        pl.BlockSpec((1, tt, d), lambda i, j: (i, j, 0)),
    )
    return pl.pallas_call(
        functools.partial(_front_prompt_kernel, tt=tt, tq=tq),
        out_shape=out_shape,
        grid=(b, t // tt),
        in_specs=_front_in_specs(1, tt, d, weights),
        out_specs=out_specs,
        compiler_params=_params("arbitrary", "arbitrary"),
        name="front_prompt",
    )(x, mod3, mod3, cos_t, sin_t, *weights)


def _front_sample(x, mod3, cos_t, sin_t, weights, *, nb, tt):
    bseq, t, d = x.shape
    out_shape = (
        jax.ShapeDtypeStruct((bseq, N_HEADS, t, KV_LORA), F32),
        jax.ShapeDtypeStruct((bseq, N_HEADS, t, QK_ROPE), F32),
        jax.ShapeDtypeStruct((bseq, t, KV_LORA), F32),
        jax.ShapeDtypeStruct((bseq, t, QK_ROPE), F32),
        jax.ShapeDtypeStruct((bseq, t, d), BF16),
    )
    out_specs = (
        pl.BlockSpec((nb, N_HEADS, tt, KV_LORA), lambda i, j: (i, 0, j, 0)),
        pl.BlockSpec((nb, N_HEADS, tt, QK_ROPE), lambda i, j: (i, 0, j, 0)),
        pl.BlockSpec((nb, tt, KV_LORA), lambda i, j: (i, j, 0)),
        pl.BlockSpec((nb, tt, QK_ROPE), lambda i, j: (i, j, 0)),
        pl.BlockSpec((nb, tt, d), lambda i, j: (i, j, 0)),
    )
    return pl.pallas_call(
        functools.partial(_front_sample_kernel, nb=nb, tt=tt),
        out_shape=out_shape,
        grid=(bseq // nb, t // tt),
        in_specs=_front_in_specs(nb, tt, d, weights),
        out_specs=out_specs,
        compiler_params=_params("arbitrary", "arbitrary"),
        name="front_sample",
    )(x, mod3, mod3, cos_t, sin_t, *weights)


def _attn_prompt_kernel(qT_ref, k_ref, kp_ref, vT_ref, w_uv_ref, o_ref, m_scr, l_scr, acc_scr, sa_scr, sb_scr,
                        *, tq, nt):
    def scores(qi, j):
        start = pl.multiple_of(j * tq, tq)
        k = k_ref[0, pl.ds(start, tq), :]
        kp = kp_ref[0, pl.ds(start, tq), :]
        return _dot(k, qT_ref[0, qi, 0:KV_LORA, :]) + _dot(kp, qT_ref[0, qi, KV_LORA:QK_CAT, :])

    def accumulate(j, sT, masked):
        if masked:
            kpos = lax.broadcasted_iota(jnp.int32, sT.shape, 0)
            qpos = lax.broadcasted_iota(jnp.int32, sT.shape, 1) & (tq - 1)
            sT = jnp.where(kpos <= qpos, sT, NEG_INF)
        m_prev = m_scr[...]
        m_new = jnp.maximum(m_prev, jnp.max(sT, axis=0, keepdims=True))
        alpha = jnp.exp2(m_prev - m_new)
        pT = jnp.exp2(sT - m_new)
        l_scr[...] = alpha * l_scr[...] + jnp.sum(pT, axis=0, keepdims=True)
        acc_scr[...] = alpha * acc_scr[...] + _dot(vT_ref[0, j], pT.astype(BF16))
        m_scr[...] = m_new

    def query_tile(qi, n_pairs, odd):
        m_scr[...] = jnp.full(m_scr.shape, NEG_INF, F32)
        l_scr[...] = jnp.zeros(l_scr.shape, F32)
        acc_scr[...] = jnp.zeros(acc_scr.shape, F32)

        sa_scr[...] = scores(qi, 0)

        def body(i, carry):
            j = 2 * i
            sb_scr[...] = scores(qi, j + 1)
            accumulate(j, sa_scr[...], False)
            sa_scr[...] = scores(qi, j + 2)
            accumulate(j + 1, sb_scr[...], False)
            return carry

        lax.fori_loop(0, n_pairs, body, 0)
        if odd:
            sb_scr[...] = scores(qi, qi)
            accumulate(qi - 1, sa_scr[...], False)
            accumulate(qi, sb_scr[...], True)
        else:
            accumulate(qi, sa_scr[...], True)

        oT = acc_scr[...] * (1.0 / l_scr[...])
        rows = pl.ds(pl.multiple_of(qi * tq, tq), tq)
        for hd in range(N_HEADS):
            o_h = oT[:, hd * tq:(hd + 1) * tq].T.astype(BF16)
            o_ref[0, rows, hd * V_DIM:(hd + 1) * V_DIM] = _dot(o_h, w_uv_ref[hd]).astype(o_ref.dtype)

    def tile_pair(p, carry):
        query_tile(2 * p, p, False)
        query_tile(2 * p + 1, p, True)
        return carry

    lax.fori_loop(0, nt // 2, tile_pair, 0)
    if nt % 2:
        query_tile(nt - 1, nt // 2, False)


def _attn_prompt(qT, ckvb, kpeb, ckvT, w_uv_p, *, tq):
    b, nt, _, cols = qT.shape
    t = nt * tq
    assert tq & (tq - 1) == 0
    return pl.pallas_call(
        functools.partial(_attn_prompt_kernel, tq=tq, nt=nt),
        out_shape=jax.ShapeDtypeStruct((b, t, N_HEADS * V_DIM), BF16),
        grid=(b,),
        in_specs=[
            pl.BlockSpec((1, nt, QK_CAT, cols), lambda i: (i, 0, 0, 0)),
            pl.BlockSpec((1, t, KV_LORA), lambda i: (i, 0, 0)),
            pl.BlockSpec((1, t, QK_ROPE), lambda i: (i, 0, 0)),
            pl.BlockSpec((1, nt, KV_LORA, tq), lambda i: (i, 0, 0, 0)),
            _full(w_uv_p.shape),
        ],
        out_specs=pl.BlockSpec((1, t, N_HEADS * V_DIM), lambda i: (i, 0, 0)),
        scratch_shapes=[
            pltpu.VMEM((1, cols), F32),
            pltpu.VMEM((1, cols), F32),
            pltpu.VMEM((KV_LORA, cols), F32),
            pltpu.VMEM((tq, cols), F32),
            pltpu.VMEM((tq, cols), F32),
        ],
        compiler_params=_params("arbitrary"),
        name="attn_prompt",
    )(qT, ckvb, kpeb, ckvT, w_uv_p)


SAMPLE_CHUNK_PAGES = 64
SAMPLE_SLOTS = 3


def _sample_stages(pt_ref, ql_ref, qp_ref, kn_ref, kpn_ref, cache_k, cache_pT, o_ref,
                   kbuf, pbuf, kb_scr, sems, *, step, n_steps, spp, n_chunks, cp, ts):
    rows = N_HEADS * ts
    nseq = n_steps * spp
    ahead = SAMPLE_SLOTS - 1

    def page_copies(seq, chunk, slot, page):
        phys = pt_ref[seq, chunk * cp + page]
        dst = pl.ds(page * PAGE_SIZE, PAGE_SIZE)
        return (
            pltpu.make_async_copy(cache_k.at[phys], kbuf.at[slot, dst], sems.at[slot, 0]),
            pltpu.make_async_copy(cache_pT.at[phys], pbuf.at[slot, :, dst], sems.at[slot, 1]),
        )

    def start_chunk(seq, chunk, slot):
        for page in range(cp):
            for c in page_copies(seq, chunk, slot, page):
                c.start()

    def wait_chunk(seq, chunk, slot):
        for page in range(cp):
            for c in page_copies(seq, chunk, slot, page):
                c.wait()

    def update(st, s, v):
        m_new = jnp.maximum(st["m"], jnp.max(s, axis=1, keepdims=True))
        alpha = jnp.exp2(st["m"] - m_new)
        p = jnp.exp2(s - m_new)
        st["l"] = alpha * st["l"] + jnp.sum(p, axis=1, keepdims=True)
        st["acc"] = alpha * st["acc"] + _dot(p.astype(BF16), v)
        st["m"] = m_new

    def make_stage(q, c, st):
        seq = step * spp + q

        def stage():
            if q == 0 and c == 0:
                @pl.when(step == 0)
                def _():
                    for c0 in range(ahead):
                        start_chunk(0, c0, c0 % SAMPLE_SLOTS)
            if c == 0:
                st["ql"] = ql_ref[q].reshape(rows, KV_LORA).astype(BF16)
                st["qp"] = qp_ref[q].reshape(rows, QK_ROPE).astype(BF16)
                st["m"] = jnp.full((rows, 1), NEG_INF, F32)
                st["l"] = jnp.zeros((rows, 1), F32)
                st["acc"] = jnp.zeros((rows, KV_LORA), F32)
            if c < n_chunks:
                slot = lax.rem(seq * n_chunks + c, SAMPLE_SLOTS)
                nxt = c + ahead
                nxt_slot = lax.rem(seq * n_chunks + nxt, SAMPLE_SLOTS)
                if nxt < n_chunks:
                    start_chunk(seq, nxt, nxt_slot)
                else:
                    @pl.when(seq + 1 < nseq)
                    def _():
                        start_chunk(seq + 1, nxt - n_chunks, nxt_slot)
                wait_chunk(seq, c, slot)
                k = kbuf[slot].astype(BF16)
                kb_scr[c % 2] = k
                s_cur = _dot_nt(st["ql"], k) + _dot(st["qp"], pbuf[slot].astype(BF16))
            if c == 0:
                kn = kn_ref[q].astype(BF16)
                kpn = kpn_ref[q].astype(BF16)
                s = _dot_nt(st["ql"], kn) + _dot_nt(st["qp"], kpn)
                s3 = s.reshape(N_HEADS, ts, ts)
                qpos = lax.broadcasted_iota(jnp.int32, s3.shape, 1)
                kpos = lax.broadcasted_iota(jnp.int32, s3.shape, 2)
                update(st, jnp.where(kpos <= qpos, s3, NEG_INF).reshape(rows, ts), kn)
            else:
                update(st, st["s_prev"], kb_scr[(c - 1) % 2])
            if c < n_chunks:
                st["s_prev"] = s_cur
            else:
                o_ref[q] = (st["acc"] / st["l"]).reshape(N_HEADS, ts, KV_LORA)

        return stage

    stages = []
    for q in range(spp):
        st = {}
        stages += [make_stage(q, c, st) for c in range(n_chunks + 1)]
    return stages


def _attn_sample_kernel(pt_ref, ql_ref, qp_ref, kn_ref, kpn_ref, cache_k, cache_pT, o_ref,
                        kbuf, pbuf, kb_scr, sems, *, n_chunks, cp, ts):
    for stage in _sample_stages(pt_ref, ql_ref, qp_ref, kn_ref, kpn_ref, cache_k, cache_pT, o_ref,
                                kbuf, pbuf, kb_scr, sems, step=pl.program_id(0), n_steps=pl.num_programs(0),
                                spp=1, n_chunks=n_chunks, cp=cp, ts=ts):
        stage()


def _attn_sample(page_table, qlat, qpe, ckv_new, kpe_new, cache_ckv, cache_kpeT):
    bs, _, ts, _ = qlat.shape
    n_pages = page_table.shape[1]
    cp = min(SAMPLE_CHUNK_PAGES, n_pages // 2)
    assert n_pages % cp == 0
    n_chunks = n_pages // cp
    assert n_chunks >= SAMPLE_SLOTS - 1
    grid_spec = pltpu.PrefetchScalarGridSpec(
        num_scalar_prefetch=1,
        grid=(bs,),
        in_specs=[
            pl.BlockSpec((1, N_HEADS, ts, KV_LORA), lambda i, pt: (i, 0, 0, 0)),
            pl.BlockSpec((1, N_HEADS, ts, QK_ROPE), lambda i, pt: (i, 0, 0, 0)),
            pl.BlockSpec((1, ts, KV_LORA), lambda i, pt: (i, 0, 0)),
            pl.BlockSpec((1, ts, QK_ROPE), lambda i, pt: (i, 0, 0)),
            pl.BlockSpec(memory_space=pl.ANY),
            pl.BlockSpec(memory_space=pl.ANY),
        ],
        out_specs=pl.BlockSpec((1, N_HEADS, ts, KV_LORA), lambda i, pt: (i, 0, 0, 0)),
        scratch_shapes=[
            pltpu.VMEM((SAMPLE_SLOTS, cp * PAGE_SIZE, KV_LORA), F32),
            pltpu.VMEM((SAMPLE_SLOTS, QK_ROPE, cp * PAGE_SIZE), F32),
            pltpu.VMEM((2, cp * PAGE_SIZE, KV_LORA), BF16),
            pltpu.SemaphoreType.DMA((SAMPLE_SLOTS, 2)),
        ],
    )
    return pl.pallas_call(
        functools.partial(_attn_sample_kernel, n_chunks=n_chunks, cp=cp, ts=ts),
        out_shape=jax.ShapeDtypeStruct((bs, N_HEADS, ts, KV_LORA), F32),
        grid_spec=grid_spec,
        compiler_params=_params("arbitrary"),
        name="attn_sample",
    )(page_table, qlat, qpe, ckv_new, kpe_new, cache_ckv, cache_kpeT)


def _back_phases(o_ref, zg_ref, x_ref, gate0_ref, shift_ref, scale_ref, gate1_ref, conv0_ref, h0_ref,
                 w_uv_ref, w_o_ref, g1_ref, w_in_ref, conv_w_ref, conv_b_ref, w_a_ref, b_a_ref,
                 w_x_ref, b_x_ref, lam_ref, w_ro_ref, gf_ref,
                 y_ref, conv_out_ref, h_out_ref,
                 xprev_scr, hs_scr, hcar_scr, *, nb, tt, o_is_value):
    j = pl.program_id(1)
    rows = nb * tt
    d = x_ref.shape[-1]
    w = lam_ref.shape[-1]
    bw = w // LRU_BLOCKS
    ng = tt // SUBLANES
    st = {}

    def layer0_out():
        zg = zg_ref[...].reshape(rows, d)
        if o_is_value:
            gated = o_ref[...].reshape(rows, d) * zg
        else:
            v = jnp.concatenate(
                [_dot(o_ref[:, hd].reshape(rows, KV_LORA).astype(BF16), w_uv_ref[hd])
                 for hd in range(N_HEADS)], axis=1)
            gated = (v * zg.astype(F32)).astype(BF16)
        out0 = _dot(gated, w_o_ref[...]).reshape(nb, tt, d)
        st["x1"] = x_ref[...] + gate0_ref[...] * out0

    def lru_in():
        h1 = _adaln(st["x1"], g1_ref[...], scale_ref[...], shift_ref[...])
        hb = h1.reshape(rows, d).astype(BF16)
        st["xb"] = _dot(hb, w_in_ref[:, 0:w])
        st["hb"] = hb

    def gate_branch():
        st["zg"] = _silu(_dot(st.pop("hb"), w_in_ref[:, w:2 * w]))

    def init_state():
        @pl.when(j == 0)
        def _():
            xprev_scr[:, 0:SUBLANES - (CONV_W - 1), :] = jnp.zeros((nb, SUBLANES - (CONV_W - 1), w), F32)
            xprev_scr[:, SUBLANES - (CONV_W - 1):SUBLANES, :] = conv0_ref[...]
            hcar_scr[...] = jnp.broadcast_to(h0_ref[...], hcar_scr.shape)

    def conv():
        xb3 = st.pop("xb").reshape(nb, tt, w)
        ext = jnp.concatenate([xprev_scr[...], xb3], axis=1).reshape(nb * (ng + 1), SUBLANES, w)
        xprev_scr[...] = xb3[:, tt - SUBLANES:tt, :]
        conv_out_ref[...] = xb3[:, tt - (CONV_W - 1):tt, :]
        gshape = (nb, ng, SUBLANES, w)
        gpos = lax.broadcasted_iota(jnp.int32, gshape, 2)
        xc = conv_b_ref[...] + xb3.reshape(gshape) * conv_w_ref[CONV_W - 1:CONV_W, :]
        for sh in range(1, CONV_W):
            rolled = pltpu.roll(ext, sh, 1).reshape(nb, ng + 1, SUBLANES, w)
            shifted = jnp.where(gpos >= sh, rolled[:, 1:], rolled[:, :ng])
            xc = xc + shifted * conv_w_ref[CONV_W - 1 - sh:CONV_W - sh, :]
        st["xc"] = xc.reshape(rows, w)

    def gates():
        xc2 = st["xc"]
        xcb = xc2.astype(BF16)
        st["t_r"] = jnp.tanh(jnp.concatenate(
            [_dot(xcb[:, n * bw:(n + 1) * bw], w_a_ref[n]) for n in range(LRU_BLOCKS)], axis=1) + b_a_ref[...])
        st["t_i"] = jnp.tanh(jnp.concatenate(
            [_dot(xcb[:, n * bw:(n + 1) * bw], w_x_ref[n]) for n in range(LRU_BLOCKS)], axis=1) + b_x_ref[...])

    def recurrence_inputs():
        neg_lam = -lam_ref[...]
        softplus = jnp.maximum(neg_lam, 0.0) + jnp.log1p(jnp.exp(-jnp.abs(neg_lam)))
        half_rate = (-0.5 * LRU_C) * softplus
        log_a = half_rate * st.pop("t_r") + half_rate
        a = jnp.exp(log_a)
        one_minus_a2 = -jnp.tanh(log_a) * (a * a + 1.0)
        st["a"] = a
        st["b"] = jnp.sqrt(one_minus_a2) * ((0.5 * st.pop("t_i") + 0.5) * st.pop("xc"))

    def scan_groups():
        a_g = st.pop("a").reshape(nb * ng, SUBLANES, w)
        b_g = st.pop("b").reshape(nb * ng, SUBLANES, w)
        tpos = lax.broadcasted_iota(jnp.int32, a_g.shape, 1)
        for sh in (1, 2, 4):
            keep = tpos >= sh
            a_prev = jnp.where(keep, pltpu.roll(a_g, sh, 1), 1.0)
            b_prev = jnp.where(keep, pltpu.roll(b_g, sh, 1), 0.0)
            b_g = a_g * b_prev + b_g
            a_g = a_g * a_prev
        st["a_g"] = a_g.reshape(nb, ng, SUBLANES, w)
        st["b_g"] = b_g.reshape(nb, ng, SUBLANES, w)

    def scan_carry():
        a_g = st.pop("a_g")
        b_g = st.pop("b_g")
        hprev = hcar_scr[...]
        for g in range(ng):
            hg = a_g[:, g] * hprev + b_g[:, g]
            hs_scr[:, g * SUBLANES:(g + 1) * SUBLANES, :] = hg
            hprev = jnp.broadcast_to(hg[:, SUBLANES - 1:SUBLANES, :], hg.shape)
        hcar_scr[...] = hprev
        h_out_ref[...] = hprev[:, 0:1, :]

    def lru_out():
        hs = hs_scr[...].reshape(rows, w)
        gated1 = (hs * st.pop("zg")).astype(BF16)
        out1 = _dot(gated1, w_ro_ref[...]).reshape(nb, tt, d)
        x2 = st.pop("x1") + gate1_ref[...] * out1
        y_ref[...] = _rms(x2, gf_ref[...])

    return [init_state, layer0_out, lru_in, conv, gates, recurrence_inputs, gate_branch, scan_groups, scan_carry,
            lru_out]


def _back_kernel(*refs, nb, tt, o_is_value):
    for phase in _back_phases(*refs, nb=nb, tt=tt, o_is_value=o_is_value):
        phase()


def _back(o, zg, x, mod0, mod1, conv0, h0, weights, *, nb, tt, o_is_value):
    bseq, t, d = x.shape
    w = conv0.shape[-1]
    if o_is_value:
        o_spec = pl.BlockSpec((nb, tt, d), lambda i, j: (i, j, 0))
    else:
        o_spec = pl.BlockSpec((nb, N_HEADS, tt, KV_LORA), lambda i, j: (i, 0, j, 0))
    in_specs = [
        o_spec,
        pl.BlockSpec((nb, tt, d), lambda i, j: (i, j, 0)),
        pl.BlockSpec((nb, tt, d), lambda i, j: (i, j, 0)),
        pl.BlockSpec((nb, 1, d), lambda i, j: (i, 0, 2)),
        pl.BlockSpec((nb, 1, d), lambda i, j: (i, 0, 0)),
        pl.BlockSpec((nb, 1, d), lambda i, j: (i, 0, 1)),
        pl.BlockSpec((nb, 1, d), lambda i, j: (i, 0, 2)),
        pl.BlockSpec((nb, CONV_W - 1, w), lambda i, j: (i, 0, 0)),
        pl.BlockSpec((nb, 1, w), lambda i, j: (i, 0, 0)),
    ] + [_full(wt.shape) for wt in weights]
    out_shape = (
        jax.ShapeDtypeStruct((bseq, t, d), F32),
        jax.ShapeDtypeStruct((bseq, CONV_W - 1, w), F32),
        jax.ShapeDtypeStruct((bseq, 1, w), F32),
    )
    out_specs = (
        pl.BlockSpec((nb, tt, d), lambda i, j: (i, j, 0)),
        pl.BlockSpec((nb, CONV_W - 1, w), lambda i, j: (i, 0, 0)),
        pl.BlockSpec((nb, 1, w), lambda i, j: (i, 0, 0)),
    )
    return pl.pallas_call(
        functools.partial(_back_kernel, nb=nb, tt=tt, o_is_value=o_is_value),
        out_shape=out_shape,
        grid=(bseq // nb, t // tt),
        in_specs=in_specs,
        out_specs=out_specs,
        scratch_shapes=[
            pltpu.VMEM((nb, SUBLANES, w), F32),
            pltpu.VMEM((nb, tt, w), F32),
            pltpu.VMEM((nb, SUBLANES, w), F32),
        ],
        compiler_params=_params("arbitrary", "arbitrary"),
        name="back_prompt" if o_is_value else "back_sample",
    )(o, zg, x, mod0, mod1, mod1, mod1, conv0, h0, *weights)


def _rope_tables(pos):
    half = QK_ROPE // 2
    inv = ROPE_THETA ** (-jnp.arange(half, dtype=F32) / half)
    ang = pos.astype(F32)[:, None] * inv[None, :]
    cos, sin = jnp.cos(ang), jnp.sin(ang)
    cos_t = jnp.tile(jnp.concatenate([cos, cos], axis=1), (1, N_HEADS))
    sin_t = jnp.tile(jnp.concatenate([-sin, sin], axis=1), (1, N_HEADS))
    return cos_t, sin_t


def _tiles(bseq, t):
    tt = min(ROW_TILE, t)
    nb = max(1, min(bseq, ROW_TILE // tt))
    assert t % tt == 0 and bseq % nb == 0 and tt % SUBLANES == 0
    return nb, tt


def kernel(x_prompt, x_sample, c_prompt, c_sample, cache_ckv, cache_kpe, page_table, state_conv, state_h,
           norm_g, w_ada, b_ada, final_g,
           a_w_in, a_g_q, a_w_qb, a_g_kv, a_w_uk, a_w_uv, a_w_o,
           r_w_in, r_conv_w, r_conv_b, r_w_a, r_b_a, r_w_x, r_b_x, r_lam, r_w_o):
    bp, tp, d = x_prompt.shape
    bs, ts, _ = x_sample.shape
    assert norm_g.shape[0] == 2 and a_w_in.shape[0] == 1 and r_w_in.shape[0] == 1
    w = r_lam.shape[-1]
    past_len = page_table.shape[1] * PAGE_SIZE

    w_in = a_w_in[0]
    c_kv, c_pe = Q_LORA + KV_LORA, Q_LORA + KV_LORA + QK_ROPE
    w_in_p = jnp.concatenate(
        [w_in[:, :c_kv], w_in[:, c_pe:], w_in[:, c_kv:c_pe], w_in[:, c_kv:c_pe]], axis=1).astype(BF16)
    w_qb = a_w_qb[0].reshape(Q_LORA, N_HEADS, QK_NOPE + QK_ROPE)
    w_qb_p = jnp.concatenate(
        [w_qb[:, :, :QK_NOPE].reshape(Q_LORA, N_HEADS * QK_NOPE),
         w_qb[:, :, QK_NOPE:].reshape(Q_LORA, N_HEADS * QK_ROPE)], axis=1).astype(BF16)
    w_uk_p = a_w_uk[0].transpose(1, 2, 0).astype(BF16)
    w_uv_p = a_w_uv[0].transpose(1, 0, 2).astype(BF16)
    row = lambda v: v.reshape(1, -1)
    front_w = (row(norm_g[0]), w_in_p, row(a_g_q[0]), w_qb_p, row(a_g_kv[0]), w_uk_p)
    back_w = (w_uv_p, a_w_o[0].astype(BF16), row(norm_g[1]), r_w_in[0].astype(BF16), r_conv_w[0],
              row(r_conv_b[0]), (0.5 * r_w_a[0]).astype(BF16), row(0.5 * r_b_a[0]),
              (0.5 * r_w_x[0]).astype(BF16), row(0.5 * r_b_x[0]),
              row(r_lam[0]), r_w_o[0].astype(BF16), row(final_g))

    mod = _modulation(jnp.concatenate([c_prompt, c_sample], axis=0), w_ada, b_ada)
    mod_p = [mod[i, :bp].reshape(bp, 1, 3 * d) for i in range(2)]
    mod_s = [mod[i, bp:].reshape(bs, 1, 3 * d) for i in range(2)]

    cos_p, sin_p = _rope_tables(jnp.arange(tp))
    cos_s, sin_s = _rope_tables(past_len + jnp.arange(ts))

    _, tq_p = _tiles(bp, tp)
    tt_p = min(PROMPT_ROW_TILE, tp)
    nb_s, tt_s = _tiles(bs, ts)

    qT_p, ckv_p, kpeT_p, ckvb_p, kpeb_p, ckvT_p, zg_p = _front_prompt(
        x_prompt, mod_p[0], cos_p, sin_p, front_w, tt=tt_p, tq=tq_p)
    qlat_s, qpe_s, ckv_s, kpe_s, zg_s = _front_sample(
        x_sample, mod_s[0], cos_s, sin_s, front_w, nb=nb_s, tt=tt_s)

    v_p = _attn_prompt(qT_p, ckvb_p, kpeb_p, ckvT_p, w_uv_p, tq=tq_p)

    o_s = _attn_sample(page_table, qlat_s, qpe_s, ckv_s, kpe_s, cache_ckv[0], jnp.swapaxes(cache_kpe[0], 1, 2))

    y_p, conv_p, h_p = _back(v_p, zg_p, x_prompt, mod_p[0], mod_p[1],
                             jnp.zeros((bp, CONV_W - 1, w), F32), jnp.zeros((bp, 1, w), F32),
                             back_w, nb=1, tt=tt_p, o_is_value=True)
    y_s, conv_s, h_s = _back(o_s, zg_s, x_sample, mod_s[0], mod_s[1],
                             state_conv[0], state_h[0].reshape(bs, 1, w),
                             back_w, nb=nb_s, tt=tt_s, o_is_value=False)

    return (y_p, y_s, ckv_p[None], jnp.swapaxes(kpeT_p, 1, 2)[None], ckv_s[None], kpe_s[None],
            conv_p[None], conv_s[None], h_p.reshape(1, bp, w), h_s.reshape(1, bs, w))
```

```python
import functools
import math

import jax
import jax.numpy as jnp
from jax import lax
from jax.experimental import pallas as pl
from jax.experimental.pallas import tpu as pltpu

F32 = jnp.float32
BF16 = jnp.bfloat16

N_HEADS = 8
QK_NOPE = 128
QK_ROPE = 64
V_DIM = 128
Q_LORA = 512
KV_LORA = 256
QK_CAT = KV_LORA + QK_ROPE
ROPE_THETA = 10000.0
SM_SCALE = 1.0 / math.sqrt(QK_NOPE + QK_ROPE)
Q_SCALE = SM_SCALE * math.log2(math.e)
NEG_INF = -1e30
LRU_BLOCKS = 4
CONV_W = 4
LRU_C = 8.0
NORM_EPS = 1e-6
PAGE_SIZE = 128

VMEM_LIMIT_BYTES = 56 * 1024 * 1024
SUBLANES = 8
ROW_TILE = 256
PROMPT_ROW_TILE = 512
FRONT_ROW_TILE = 1024


def _dot(a, b):
    return jnp.dot(a, b, preferred_element_type=F32)


def _dot_nt(a, b):
    return lax.dot_general(a, b, (((1,), (1,)), ((), ())), preferred_element_type=F32)


def _rms(x, g):
    return x * lax.rsqrt(jnp.mean(x * x, axis=-1, keepdims=True) + NORM_EPS) * g


def _adaln(x, g, scale, shift):
    r = lax.rsqrt(jnp.mean(x * x, axis=-1, keepdims=True) + NORM_EPS)
    return x * r * (g * (1.0 + scale)) + shift


def _silu(x):
    half = 0.5 * x
    return half * jnp.tanh(half) + half


def _params(*sem):
    return pltpu.CompilerParams(dimension_semantics=sem, vmem_limit_bytes=VMEM_LIMIT_BYTES)


def _full(shape):
    return pl.BlockSpec(shape, lambda *_: (0,) * len(shape))


def _mod_kernel(c_ref, w_ref, b_ref, o_ref):
    o_ref[0] = _dot(c_ref[...].astype(BF16), w_ref[0].astype(BF16)) + b_ref[0]


def _modulation(c_all, w_ada, b_ada):
    depth, d, d3 = w_ada.shape
    n = c_all.shape[0]
    return pl.pallas_call(
        _mod_kernel,
        out_shape=jax.ShapeDtypeStruct((depth, n, d3), F32),
        grid=(depth, d3 // d),
        in_specs=[
            pl.BlockSpec((n, d), lambda i, j: (0, 0)),
            pl.BlockSpec((1, d, d), lambda i, j: (i, 0, j)),
            pl.BlockSpec((1, 1, d), lambda i, j: (i, 0, j)),
        ],
        out_specs=pl.BlockSpec((1, n, d), lambda i, j: (i, 0, j)),
        compiler_params=_params("arbitrary", "arbitrary"),
        name="mod",
    )(c_all, w_ada, b_ada.reshape(depth, 1, d3))


def _rope_rows(x, cos, sin_signed, nb, tt):
    width = x.shape[-1]
    lane = lax.broadcasted_iota(jnp.int32, x.shape, 1)
    first_half = (lane & (QK_ROPE - 1)) < (QK_ROPE // 2)
    partner = jnp.where(first_half,
                        pltpu.roll(x, width - QK_ROPE // 2, 1),
                        pltpu.roll(x, QK_ROPE // 2, 1))
    x3 = x.reshape(nb, tt, width)
    p3 = partner.reshape(nb, tt, width)
    return x3 * cos[None] + p3 * sin_signed[None]


def _front_compute(x_ref, shift_ref, scale_ref, cos_ref, sin_ref, g_ref, w_in_ref, gq_ref, w_qb_ref, gkv_ref,
                   nb, tt):
    rows = nb * tt
    d = x_ref.shape[-1]
    h = _adaln(x_ref[...], g_ref[...], scale_ref[...], shift_ref[...])
    hb = h.reshape(rows, d).astype(BF16)
    c0, c1, c2, c3 = Q_LORA, Q_LORA + KV_LORA, Q_LORA + KV_LORA + d, Q_LORA + KV_LORA + d + 2 * QK_ROPE
    u_q = _dot(hb, w_in_ref[:, 0:c0])
    u_kv = _dot(hb, w_in_ref[:, c0:c1])
    u_kpe = _dot(hb, w_in_ref[:, c2:c3])
    zg = lambda: _silu(_dot(hb, w_in_ref[:, c1:c2]))
    ckv = _rms(u_kv, gkv_ref[...])
    cos = cos_ref[...]
    sin = sin_ref[...]
    kr = _rope_rows(u_kpe, cos[:, 0:2 * QK_ROPE], sin[:, 0:2 * QK_ROPE], nb, tt)
    qn = _rms(u_q, gq_ref[...]).astype(BF16)
    q = _dot(qn, w_qb_ref[...])
    return zg, ckv, kr, q, cos, sin


def _front_prompt_kernel(x_ref, shift_ref, scale_ref, cos_ref, sin_ref, g_ref, w_in_ref, gq_ref, w_qb_ref,
                         gkv_ref, w_uk_ref,
                         qT_ref, ckv_ref, kpeT_ref, ckvb_ref, kpeb_ref, ckvT_ref, zg_ref, *, tt, tq):
    nope_w = N_HEADS * QK_NOPE
    pe_w = N_HEADS * QK_ROPE
    zg, ckv, kr, q, cos, sin = _front_compute(
        x_ref, shift_ref, scale_ref, cos_ref, sin_ref, g_ref, w_in_ref, gq_ref, w_qb_ref, gkv_ref, 1, tt)
    ckv_ref[0] = ckv
    ckvb_ref[0] = ckv.astype(ckvb_ref.dtype)
    ckvT = ckv.T.astype(ckvT_ref.dtype)
    kr2 = kr[0]
    kpeb_ref[0] = kr2[:, 0:QK_ROPE].astype(kpeb_ref.dtype)
    kpeT_ref[0] = kr2.T[0:QK_ROPE, :]
    tiles = [(s, slice(s * tq, (s + 1) * tq)) for s in range(tt // tq)]
    for s, tok in tiles:
        ckvT_ref[0, s] = ckvT[:, tok]
    qrT = (_rope_rows(q[:, nope_w:nope_w + pe_w], cos, sin, 1, tt)[0] * Q_SCALE).T
    for hd in range(N_HEADS):
        col = slice(hd * tq, (hd + 1) * tq)
        qh = q[:, hd * QK_NOPE:(hd + 1) * QK_NOPE].astype(BF16)
        qlT = (_dot(qh, w_uk_ref[hd]) * Q_SCALE).T
        for s, tok in tiles:
            qT_ref[0, s, 0:KV_LORA, col] = qlT[:, tok].astype(qT_ref.dtype)
            qT_ref[0, s, KV_LORA:QK_CAT, col] = qrT[hd * QK_ROPE:(hd + 1) * QK_ROPE, tok].astype(qT_ref.dtype)
    zg_ref[0] = zg().astype(zg_ref.dtype)


def _front_sample_kernel(x_ref, shift_ref, scale_ref, cos_ref, sin_ref, g_ref, w_in_ref, gq_ref, w_qb_ref,
                         gkv_ref, w_uk_ref,
                         qlat_ref, qpe_ref, ckv_ref, kpe_ref, zg_ref, *, nb, tt):
    d = x_ref.shape[-1]
    nope_w = N_HEADS * QK_NOPE
    pe_w = N_HEADS * QK_ROPE
    zg, ckv, kr, q, cos, sin = _front_compute(
        x_ref, shift_ref, scale_ref, cos_ref, sin_ref, g_ref, w_in_ref, gq_ref, w_qb_ref, gkv_ref, nb, tt)
    ckv_ref[...] = ckv.reshape(nb, tt, KV_LORA)
    kpe_ref[...] = kr[:, :, 0:QK_ROPE]
    for hd in range(N_HEADS):
        qh = q[:, hd * QK_NOPE:(hd + 1) * QK_NOPE].astype(BF16)
        ql = _dot(qh, w_uk_ref[hd]) * Q_SCALE
        qlat_ref[:, hd] = ql.reshape(nb, tt, KV_LORA)
    qr = _rope_rows(q[:, nope_w:nope_w + pe_w], cos, sin, nb, tt) * Q_SCALE
    for hd in range(N_HEADS):
        qpe_ref[:, hd] = qr[:, :, hd * QK_ROPE:(hd + 1) * QK_ROPE]
    zg_ref[...] = zg().reshape(nb, tt, d).astype(zg_ref.dtype)


def _front_in_specs(nb, tt, d, weights):
    pe_w = N_HEADS * QK_ROPE
    return [
        pl.BlockSpec((nb, tt, d), lambda i, j: (i, j, 0)),
        pl.BlockSpec((nb, 1, d), lambda i, j: (i, 0, 0)),
        pl.BlockSpec((nb, 1, d), lambda i, j: (i, 0, 1)),
        pl.BlockSpec((tt, pe_w), lambda i, j: (j, 0)),
        pl.BlockSpec((tt, pe_w), lambda i, j: (j, 0)),
    ] + [_full(w.shape) for w in weights]


def _front_prompt(x, mod3, cos_t, sin_t, weights, *, tt, tq):
    b, t, d = x.shape
    assert tt % tq == 0 and t % tt == 0
    nq = tt // tq
    nt = t // tq
    out_shape = (
        jax.ShapeDtypeStruct((b, nt, QK_CAT, N_HEADS * tq), BF16),
        jax.ShapeDtypeStruct((b, t, KV_LORA), F32),
        jax.ShapeDtypeStruct((b, QK_ROPE, t), F32),
        jax.ShapeDtypeStruct((b, t, KV_LORA), BF16),
        jax.ShapeDtypeStruct((b, t, QK_ROPE), BF16),
        jax.ShapeDtypeStruct((b, nt, KV_LORA, tq), BF16),
        jax.ShapeDtypeStruct((b, t, d), BF16),
    )
    out_specs = (
        pl.BlockSpec((1, nq, QK_CAT, N_HEADS * tq), lambda i, j: (i, j, 0, 0)),
        pl.BlockSpec((1, tt, KV_LORA), lambda i, j: (i, j, 0)),
        pl.BlockSpec((1, QK_ROPE, tt), lambda i, j: (i, 0, j)),
        pl.BlockSpec((1, tt, KV_LORA), lambda i, j: (i, j, 0)),
        pl.BlockSpec((1, tt, QK_ROPE), lambda i, j: (i, j, 0)),
        pl.BlockSpec((1, nq, KV_LORA, tq), lambda i, j: (i, j, 0, 0)),
        pl.BlockSpec((1, tt, d), lambda i, j: (i, j, 0)),
    )
    return pl.pallas_call(
        functools.partial(_front_prompt_kernel, tt=tt, tq=tq),
        out_shape=out_shape,
        grid=(b, t // tt),
        in_specs=_front_in_specs(1, tt, d, weights),
        out_specs=out_specs,
        compiler_params=_params("arbitrary", "arbitrary"),
        name="front_prompt",
    )(x, mod3, mod3, cos_t, sin_t, *weights)


def _front_sample(x, mod3, cos_t, sin_t, weights, *, nb, tt):
    bseq, t, d = x.shape
    out_shape = (
        jax.ShapeDtypeStruct((bseq, N_HEADS, t, KV_LORA), F32),
        jax.ShapeDtypeStruct((bseq, N_HEADS, t, QK_ROPE), F32),
        jax.ShapeDtypeStruct((bseq, t, KV_LORA), F32),
        jax.ShapeDtypeStruct((bseq, t, QK_ROPE), F32),
        jax.ShapeDtypeStruct((bseq, t, d), BF16),
    )
    out_specs = (
        pl.BlockSpec((nb, N_HEADS, tt, KV_LORA), lambda i, j: (i, 0, j, 0)),
        pl.BlockSpec((nb, N_HEADS, tt, QK_ROPE), lambda i, j: (i, 0, j, 0)),
        pl.BlockSpec((nb, tt, KV_LORA), lambda i, j: (i, j, 0)),
        pl.BlockSpec((nb, tt, QK_ROPE), lambda i, j: (i, j, 0)),
        pl.BlockSpec((nb, tt, d), lambda i, j: (i, j, 0)),
    )
    return pl.pallas_call(
        functools.partial(_front_sample_kernel, nb=nb, tt=tt),
        out_shape=out_shape,
        grid=(bseq // nb, t // tt),
        in_specs=_front_in_specs(nb, tt, d, weights),
        out_specs=out_specs,
        compiler_params=_params("arbitrary", "arbitrary"),
        name="front_sample",
    )(x, mod3, mod3, cos_t, sin_t, *weights)


def _attn_prompt_kernel(qT_ref, k_ref, kp_ref, vT_ref, w_uv_ref, o_ref, m_scr, l_scr, acc_scr, sa_scr, sb_scr,
                        *, tq, nt):
    def scores(qi, j):
        start = pl.multiple_of(j * tq, tq)
        k = k_ref[0, pl.ds(start, tq), :]
        kp = kp_ref[0, pl.ds(start, tq), :]
        return _dot(k, qT_ref[0, qi, 0:KV_LORA, :]) + _dot(kp, qT_ref[0, qi, KV_LORA:QK_CAT, :])

    def accumulate(j, sT, masked):
        if masked:
            kpos = lax.broadcasted_iota(jnp.int32, sT.shape, 0)
            qpos = lax.broadcasted_iota(jnp.int32, sT.shape, 1) & (tq - 1)
            sT = jnp.where(kpos <= qpos, sT, NEG_INF)
        m_prev = m_scr[...]
        m_new = jnp.maximum(m_prev, jnp.max(sT, axis=0, keepdims=True))
        alpha = jnp.exp2(m_prev - m_new)
        pT = jnp.exp2(sT - m_new)
        l_scr[...] = alpha * l_scr[...] + jnp.sum(pT, axis=0, keepdims=True)
        acc_scr[...] = alpha * acc_scr[...] + _dot(vT_ref[0, j], pT.astype(BF16))
        m_scr[...] = m_new

    def query_tile(qi, n_pairs, odd):
        m_scr[...] = jnp.full(m_scr.shape, NEG_INF, F32)
        l_scr[...] = jnp.zeros(l_scr.shape, F32)
        acc_scr[...] = jnp.zeros(acc_scr.shape, F32)

        sa_scr[...] = scores(qi, 0)

        def body(i, carry):
            j = 2 * i
            sb_scr[...] = scores(qi, j + 1)
            accumulate(j, sa_scr[...], False)
            sa_scr[...] = scores(qi, j + 2)
            accumulate(j + 1, sb_scr[...], False)
            return carry

        lax.fori_loop(0, n_pairs, body, 0)
        if odd:
            sb_scr[...] = scores(qi, qi)
            accumulate(qi - 1, sa_scr[...], False)
            accumulate(qi, sb_scr[...], True)
        else:
            accumulate(qi, sa_scr[...], True)

        oT = acc_scr[...] * (1.0 / l_scr[...])
        rows = pl.ds(pl.multiple_of(qi * tq, tq), tq)
        for hd in range(N_HEADS):
            o_h = oT[:, hd * tq:(hd + 1) * tq].T.astype(BF16)
            o_ref[0, rows, hd * V_DIM:(hd + 1) * V_DIM] = _dot(o_h, w_uv_ref[hd]).astype(o_ref.dtype)

    for qi in range(nt):
        query_tile(qi, qi // 2, qi % 2 == 1)


def _attn_prompt(qT, ckvb, kpeb, ckvT, w_uv_p, *, tq):
    b, nt, _, cols = qT.shape
    t = nt * tq
    assert tq & (tq - 1) == 0
    return pl.pallas_call(
        functools.partial(_attn_prompt_kernel, tq=tq, nt=nt),
        out_shape=jax.ShapeDtypeStruct((b, t, N_HEADS * V_DIM), BF16),
        grid=(b,),
        in_specs=[
            pl.BlockSpec((1, nt, QK_CAT, cols), lambda i: (i, 0, 0, 0)),
            pl.BlockSpec((1, t, KV_LORA), lambda i: (i, 0, 0)),
            pl.BlockSpec((1, t, QK_ROPE), lambda i: (i, 0, 0)),
            pl.BlockSpec((1, nt, KV_LORA, tq), lambda i: (i, 0, 0, 0)),
            _full(w_uv_p.shape),
        ],
        out_specs=pl.BlockSpec((1, t, N_HEADS * V_DIM), lambda i: (i, 0, 0)),
        scratch_shapes=[
            pltpu.VMEM((1, cols), F32),
            pltpu.VMEM((1, cols), F32),
            pltpu.VMEM((KV_LORA, cols), F32),
            pltpu.VMEM((tq, cols), F32),
            pltpu.VMEM((tq, cols), F32),
        ],
        compiler_params=_params("arbitrary"),
        name="attn_prompt",
    )(qT, ckvb, kpeb, ckvT, w_uv_p)


SAMPLE_CHUNK_PAGES = 64
SAMPLE_SLOTS = 3


def _sample_stages(pt_ref, ql_ref, qp_ref, kn_ref, kpn_ref, cache_k, cache_pT, o_ref,
                   kbuf, pbuf, kb_scr, sems, *, step, n_steps, spp, n_chunks, cp, ts):
    rows = N_HEADS * ts
    nseq = n_steps * spp
    ahead = SAMPLE_SLOTS - 1

    def page_copies(seq, chunk, slot, page):
        phys = pt_ref[seq, chunk * cp + page]
        dst = pl.ds(page * PAGE_SIZE, PAGE_SIZE)
        return (
            pltpu.make_async_copy(cache_k.at[phys], kbuf.at[slot, dst], sems.at[slot, 0]),
            pltpu.make_async_copy(cache_pT.at[phys], pbuf.at[slot, :, dst], sems.at[slot, 1]),
        )

    def start_chunk(seq, chunk, slot):
        for page in range(cp):
            for c in page_copies(seq, chunk, slot, page):
                c.start()

    def wait_chunk(seq, chunk, slot):
        for page in range(cp):
            for c in page_copies(seq, chunk, slot, page):
                c.wait()

    def update(st, s, v):
        m_new = jnp.maximum(st["m"], jnp.max(s, axis=1, keepdims=True))
        alpha = jnp.exp2(st["m"] - m_new)
        p = jnp.exp2(s - m_new)
        st["l"] = alpha * st["l"] + jnp.sum(p, axis=1, keepdims=True)
        st["acc"] = alpha * st["acc"] + _dot(p.astype(BF16), v)
        st["m"] = m_new

    def make_stage(q, c, st):
        seq = step * spp + q

        def stage():
            if q == 0 and c == 0:
                @pl.when(step == 0)
                def _():
                    for c0 in range(ahead):
                        start_chunk(0, c0, c0 % SAMPLE_SLOTS)
            if c == 0:
                st["ql"] = ql_ref[q].reshape(rows, KV_LORA).astype(BF16)
                st["qp"] = qp_ref[q].reshape(rows, QK_ROPE).astype(BF16)
                st["m"] = jnp.full((rows, 1), NEG_INF, F32)
                st["l"] = jnp.zeros((rows, 1), F32)
                st["acc"] = jnp.zeros((rows, KV_LORA), F32)
            if c < n_chunks:
                slot = lax.rem(seq * n_chunks + c, SAMPLE_SLOTS)
                nxt = c + ahead
                nxt_slot = lax.rem(seq * n_chunks + nxt, SAMPLE_SLOTS)
                if nxt < n_chunks:
                    start_chunk(seq, nxt, nxt_slot)
                else:
                    @pl.when(seq + 1 < nseq)
                    def _():
                        start_chunk(seq + 1, nxt - n_chunks, nxt_slot)
                wait_chunk(seq, c, slot)
                k = kbuf[slot].astype(BF16)
                kb_scr[c % 2] = k
                s_cur = _dot_nt(st["ql"], k) + _dot(st["qp"], pbuf[slot].astype(BF16))
            if c == 0:
                kn = kn_ref[q].astype(BF16)
                kpn = kpn_ref[q].astype(BF16)
                s = _dot_nt(st["ql"], kn) + _dot_nt(st["qp"], kpn)
                s3 = s.reshape(N_HEADS, ts, ts)
                qpos = lax.broadcasted_iota(jnp.int32, s3.shape, 1)
                kpos = lax.broadcasted_iota(jnp.int32, s3.shape, 2)
                update(st, jnp.where(kpos <= qpos, s3, NEG_INF).reshape(rows, ts), kn)
            else:
                update(st, st["s_prev"], kb_scr[(c - 1) % 2])
            if c < n_chunks:
                st["s_prev"] = s_cur
            else:
                o_ref[q] = (st["acc"] / st["l"]).reshape(N_HEADS, ts, KV_LORA)

        return stage

    stages = []
    for q in range(spp):
        st = {}
        stages += [make_stage(q, c, st) for c in range(n_chunks + 1)]
    return stages


def _attn_sample_kernel(pt_ref, ql_ref, qp_ref, kn_ref, kpn_ref, cache_k, cache_pT, o_ref,
                        kbuf, pbuf, kb_scr, sems, *, n_chunks, cp, ts):
    for stage in _sample_stages(pt_ref, ql_ref, qp_ref, kn_ref, kpn_ref, cache_k, cache_pT, o_ref,
                                kbuf, pbuf, kb_scr, sems, step=pl.program_id(0), n_steps=pl.num_programs(0),
                                spp=1, n_chunks=n_chunks, cp=cp, ts=ts):
        stage()


def _attn_sample(page_table, qlat, qpe, ckv_new, kpe_new, cache_ckv, cache_kpeT):
    bs, _, ts, _ = qlat.shape
    n_pages = page_table.shape[1]
    cp = min(SAMPLE_CHUNK_PAGES, n_pages // 2)
    assert n_pages % cp == 0
    n_chunks = n_pages // cp
    assert n_chunks >= SAMPLE_SLOTS - 1
    grid_spec = pltpu.PrefetchScalarGridSpec(
        num_scalar_prefetch=1,
        grid=(bs,),
        in_specs=[
            pl.BlockSpec((1, N_HEADS, ts, KV_LORA), lambda i, pt: (i, 0, 0, 0)),
            pl.BlockSpec((1, N_HEADS, ts, QK_ROPE), lambda i, pt: (i, 0, 0, 0)),
            pl.BlockSpec((1, ts, KV_LORA), lambda i, pt: (i, 0, 0)),
            pl.BlockSpec((1, ts, QK_ROPE), lambda i, pt: (i, 0, 0)),
            pl.BlockSpec(memory_space=pl.ANY),
            pl.BlockSpec(memory_space=pl.ANY),
        ],
        out_specs=pl.BlockSpec((1, N_HEADS, ts, KV_LORA), lambda i, pt: (i, 0, 0, 0)),
        scratch_shapes=[
            pltpu.VMEM((SAMPLE_SLOTS, cp * PAGE_SIZE, KV_LORA), F32),
            pltpu.VMEM((SAMPLE_SLOTS, QK_ROPE, cp * PAGE_SIZE), F32),
            pltpu.VMEM((2, cp * PAGE_SIZE, KV_LORA), BF16),
            pltpu.SemaphoreType.DMA((SAMPLE_SLOTS, 2)),
        ],
    )
    return pl.pallas_call(
        functools.partial(_attn_sample_kernel, n_chunks=n_chunks, cp=cp, ts=ts),
        out_shape=jax.ShapeDtypeStruct((bs, N_HEADS, ts, KV_LORA), F32),
        grid_spec=grid_spec,
        compiler_params=_params("arbitrary"),
        name="attn_sample",
    )(page_table, qlat, qpe, ckv_new, kpe_new, cache_ckv, cache_kpeT)


def _back_phases(o_ref, zg_ref, x_ref, gate0_ref, shift_ref, scale_ref, gate1_ref, conv0_ref, h0_ref,
                 w_uv_ref, w_o_ref, g1_ref, w_in_ref, conv_w_ref, conv_b_ref, w_a_ref, b_a_ref,
                 w_x_ref, b_x_ref, lam_ref, w_ro_ref, gf_ref,
                 y_ref, conv_out_ref, h_out_ref,
                 xprev_scr, hs_scr, hcar_scr, *, nb, tt, o_is_value):
    j = pl.program_id(1)
    rows = nb * tt
    d = x_ref.shape[-1]
    w = lam_ref.shape[-1]
    bw = w // LRU_BLOCKS
    ng = tt // SUBLANES
    st = {}

    def layer0_out():
        zg = zg_ref[...].reshape(rows, d)
        if o_is_value:
            gated = o_ref[...].reshape(rows, d) * zg
        else:
            v = jnp.concatenate(
                [_dot(o_ref[:, hd].reshape(rows, KV_LORA).astype(BF16), w_uv_ref[hd])
                 for hd in range(N_HEADS)], axis=1)
            gated = (v * zg.astype(F32)).astype(BF16)
        out0 = _dot(gated, w_o_ref[...]).reshape(nb, tt, d)
        st["x1"] = x_ref[...] + gate0_ref[...] * out0

    def lru_in():
        h1 = _adaln(st["x1"], g1_ref[...], scale_ref[...], shift_ref[...])
        hb = h1.reshape(rows, d).astype(BF16)
        st["xb"] = _dot(hb, w_in_ref[:, 0:w])
        st["hb"] = hb

    def gate_branch():
        st["zg"] = _silu(_dot(st.pop("hb"), w_in_ref[:, w:2 * w]))

    def init_state():
        @pl.when(j == 0)
        def _():
            xprev_scr[:, 0:SUBLANES - (CONV_W - 1), :] = jnp.zeros((nb, SUBLANES - (CONV_W - 1), w), F32)
            xprev_scr[:, SUBLANES - (CONV_W - 1):SUBLANES, :] = conv0_ref[...]
            hcar_scr[...] = jnp.broadcast_to(h0_ref[...], hcar_scr.shape)

    def conv():
        xb3 = st.pop("xb").reshape(nb, tt, w)
        ext = jnp.concatenate([xprev_scr[...], xb3], axis=1).reshape(nb * (ng + 1), SUBLANES, w)
        xprev_scr[...] = xb3[:, tt - SUBLANES:tt, :]
        conv_out_ref[...] = xb3[:, tt - (CONV_W - 1):tt, :]
        gshape = (nb, ng, SUBLANES, w)
        gpos = lax.broadcasted_iota(jnp.int32, gshape, 2)
        xc = conv_b_ref[...] + xb3.reshape(gshape) * conv_w_ref[CONV_W - 1:CONV_W, :]
        for sh in range(1, CONV_W):
            rolled = pltpu.roll(ext, sh, 1).reshape(nb, ng + 1, SUBLANES, w)
            shifted = jnp.where(gpos >= sh, rolled[:, 1:], rolled[:, :ng])
            xc = xc + shifted * conv_w_ref[CONV_W - 1 - sh:CONV_W - sh, :]
        st["xc"] = xc.reshape(rows, w)

    def gates():
        xc2 = st["xc"]
        xcb = xc2.astype(BF16)
        st["t_r"] = jnp.tanh(jnp.concatenate(
            [_dot(xcb[:, n * bw:(n + 1) * bw], w_a_ref[n]) for n in range(LRU_BLOCKS)], axis=1) + b_a_ref[...])
        st["t_i"] = jnp.tanh(jnp.concatenate(
            [_dot(xcb[:, n * bw:(n + 1) * bw], w_x_ref[n]) for n in range(LRU_BLOCKS)], axis=1) + b_x_ref[...])

    def recurrence_inputs():
        neg_lam = -lam_ref[...]
        softplus = jnp.maximum(neg_lam, 0.0) + jnp.log1p(jnp.exp(-jnp.abs(neg_lam)))
        half_rate = (-0.5 * LRU_C) * softplus
        log_a = half_rate * st.pop("t_r") + half_rate
        a = jnp.exp(log_a)
        one_minus_a2 = -jnp.tanh(log_a) * (a * a + 1.0)
        st["a"] = a
        st["b"] = jnp.sqrt(one_minus_a2) * ((0.5 * st.pop("t_i") + 0.5) * st.pop("xc"))

    def scan_groups():
        a_g = st.pop("a").reshape(nb * ng, SUBLANES, w)
        b_g = st.pop("b").reshape(nb * ng, SUBLANES, w)
        tpos = lax.broadcasted_iota(jnp.int32, a_g.shape, 1)
        for sh in (1, 2, 4):
            keep = tpos >= sh
            a_prev = jnp.where(keep, pltpu.roll(a_g, sh, 1), 1.0)
            b_prev = jnp.where(keep, pltpu.roll(b_g, sh, 1), 0.0)
            b_g = a_g * b_prev + b_g
            a_g = a_g * a_prev
        st["a_g"] = a_g.reshape(nb, ng, SUBLANES, w)
        st["b_g"] = b_g.reshape(nb, ng, SUBLANES, w)

    def scan_carry():
        a_g = st.pop("a_g")
        b_g = st.pop("b_g")
        hprev = hcar_scr[...]
        for g in range(ng):
            hg = a_g[:, g] * hprev + b_g[:, g]
            hs_scr[:, g * SUBLANES:(g + 1) * SUBLANES, :] = hg
            hprev = jnp.broadcast_to(hg[:, SUBLANES - 1:SUBLANES, :], hg.shape)
        hcar_scr[...] = hprev
        h_out_ref[...] = hprev[:, 0:1, :]

    def lru_out():
        hs = hs_scr[...].reshape(rows, w)
        gated1 = (hs * st.pop("zg")).astype(BF16)
        out1 = _dot(gated1, w_ro_ref[...]).reshape(nb, tt, d)
        x2 = st.pop("x1") + gate1_ref[...] * out1
        y_ref[...] = _rms(x2, gf_ref[...])

    return [init_state, layer0_out, lru_in, conv, gates, recurrence_inputs, gate_branch, scan_groups, scan_carry,
            lru_out]


def _back_kernel(*refs, nb, tt, o_is_value):
    for phase in _back_phases(*refs, nb=nb, tt=tt, o_is_value=o_is_value):
        phase()


def _back(o, zg, x, mod0, mod1, conv0, h0, weights, *, nb, tt, o_is_value):
    bseq, t, d = x.shape
    w = conv0.shape[-1]
    if o_is_value:
        o_spec = pl.BlockSpec((nb, tt, d), lambda i, j: (i, j, 0))
    else:
        o_spec = pl.BlockSpec((nb, N_HEADS, tt, KV_LORA), lambda i, j: (i, 0, j, 0))
    in_specs = [
        o_spec,
        pl.BlockSpec((nb, tt, d), lambda i, j: (i, j, 0)),
        pl.BlockSpec((nb, tt, d), lambda i, j: (i, j, 0)),
        pl.BlockSpec((nb, 1, d), lambda i, j: (i, 0, 2)),
        pl.BlockSpec((nb, 1, d), lambda i, j: (i, 0, 0)),
        pl.BlockSpec((nb, 1, d), lambda i, j: (i, 0, 1)),
        pl.BlockSpec((nb, 1, d), lambda i, j: (i, 0, 2)),
        pl.BlockSpec((nb, CONV_W - 1, w), lambda i, j: (i, 0, 0)),
        pl.BlockSpec((nb, 1, w), lambda i, j: (i, 0, 0)),
    ] + [_full(wt.shape) for wt in weights]
    out_shape = (
        jax.ShapeDtypeStruct((bseq, t, d), F32),
        jax.ShapeDtypeStruct((bseq, CONV_W - 1, w), F32),
        jax.ShapeDtypeStruct((bseq, 1, w), F32),
    )
    out_specs = (
        pl.BlockSpec((nb, tt, d), lambda i, j: (i, j, 0)),
        pl.BlockSpec((nb, CONV_W - 1, w), lambda i, j: (i, 0, 0)),
        pl.BlockSpec((nb, 1, w), lambda i, j: (i, 0, 0)),
    )
    return pl.pallas_call(
        functools.partial(_back_kernel, nb=nb, tt=tt, o_is_value=o_is_value),
        out_shape=out_shape,
        grid=(bseq // nb, t // tt),
        in_specs=in_specs,
        out_specs=out_specs,
        scratch_shapes=[
            pltpu.VMEM((nb, SUBLANES, w), F32),
            pltpu.VMEM((nb, tt, w), F32),
            pltpu.VMEM((nb, SUBLANES, w), F32),
        ],
        compiler_params=_params("arbitrary", "arbitrary"),
        name="back_prompt" if o_is_value else "back_sample",
    )(o, zg, x, mod0, mod1, mod1, mod1, conv0, h0, *weights)


def _rope_tables(pos):
    half = QK_ROPE // 2
    inv = ROPE_THETA ** (-jnp.arange(half, dtype=F32) / half)
    ang = pos.astype(F32)[:, None] * inv[None, :]
    cos, sin = jnp.cos(ang), jnp.sin(ang)
    cos_t = jnp.tile(jnp.concatenate([cos, cos], axis=1), (1, N_HEADS))
    sin_t = jnp.tile(jnp.concatenate([-sin, sin], axis=1), (1, N_HEADS))
    return cos_t, sin_t


def _tiles(bseq, t):
    tt = min(ROW_TILE, t)
    nb = max(1, min(bseq, ROW_TILE // tt))
    assert t % tt == 0 and bseq % nb == 0 and tt % SUBLANES == 0
    return nb, tt


def kernel(x_prompt, x_sample, c_prompt, c_sample, cache_ckv, cache_kpe, page_table, state_conv, state_h,
           norm_g, w_ada, b_ada, final_g,
           a_w_in, a_g_q, a_w_qb, a_g_kv, a_w_uk, a_w_uv, a_w_o,
           r_w_in, r_conv_w, r_conv_b, r_w_a, r_b_a, r_w_x, r_b_x, r_lam, r_w_o):
    bp, tp, d = x_prompt.shape
    bs, ts, _ = x_sample.shape
    assert norm_g.shape[0] == 2 and a_w_in.shape[0] == 1 and r_w_in.shape[0] == 1
    w = r_lam.shape[-1]
    past_len = page_table.shape[1] * PAGE_SIZE

    w_in = a_w_in[0]
    c_kv, c_pe = Q_LORA + KV_LORA, Q_LORA + KV_LORA + QK_ROPE
    w_in_p = jnp.concatenate(
        [w_in[:, :c_kv], w_in[:, c_pe:], w_in[:, c_kv:c_pe], w_in[:, c_kv:c_pe]], axis=1).astype(BF16)
    w_qb = a_w_qb[0].reshape(Q_LORA, N_HEADS, QK_NOPE + QK_ROPE)
    w_qb_p = jnp.concatenate(
        [w_qb[:, :, :QK_NOPE].reshape(Q_LORA, N_HEADS * QK_NOPE),
         w_qb[:, :, QK_NOPE:].reshape(Q_LORA, N_HEADS * QK_ROPE)], axis=1).astype(BF16)
    w_uk_p = a_w_uk[0].transpose(1, 2, 0).astype(BF16)
    w_uv_p = a_w_uv[0].transpose(1, 0, 2).astype(BF16)
    row = lambda v: v.reshape(1, -1)
    front_w = (row(norm_g[0]), w_in_p, row(a_g_q[0]), w_qb_p, row(a_g_kv[0]), w_uk_p)
    back_w = (w_uv_p, a_w_o[0].astype(BF16), row(norm_g[1]), r_w_in[0].astype(BF16), r_conv_w[0],
              row(r_conv_b[0]), (0.5 * r_w_a[0]).astype(BF16), row(0.5 * r_b_a[0]),
              (0.5 * r_w_x[0]).astype(BF16), row(0.5 * r_b_x[0]),
              row(r_lam[0]), r_w_o[0].astype(BF16), row(final_g))

    mod = _modulation(jnp.concatenate([c_prompt, c_sample], axis=0), w_ada, b_ada)
    mod_p = [mod[i, :bp].reshape(bp, 1, 3 * d) for i in range(2)]
    mod_s = [mod[i, bp:].reshape(bs, 1, 3 * d) for i in range(2)]

    cos_p, sin_p = _rope_tables(jnp.arange(tp))
    cos_s, sin_s = _rope_tables(past_len + jnp.arange(ts))

    _, tq_p = _tiles(bp, tp)
    tt_p = min(PROMPT_ROW_TILE, tp)
    nb_s, tt_s = _tiles(bs, ts)

    qT_p, ckv_p, kpeT_p, ckvb_p, kpeb_p, ckvT_p, zg_p = _front_prompt(
        x_prompt, mod_p[0], cos_p, sin_p, front_w, tt=min(FRONT_ROW_TILE, tp), tq=tq_p)
    qlat_s, qpe_s, ckv_s, kpe_s, zg_s = _front_sample(
        x_sample, mod_s[0], cos_s, sin_s, front_w, nb=nb_s, tt=tt_s)

    v_p = _attn_prompt(qT_p, ckvb_p, kpeb_p, ckvT_p, w_uv_p, tq=tq_p)

    o_s = _attn_sample(page_table, qlat_s, qpe_s, ckv_s, kpe_s, cache_ckv[0], jnp.swapaxes(cache_kpe[0], 1, 2))

    y_p, conv_p, h_p = _back(v_p, zg_p, x_prompt, mod_p[0], mod_p[1],
                             jnp.zeros((bp, CONV_W - 1, w), F32), jnp.zeros((bp, 1, w), F32),
                             back_w, nb=1, tt=tt_p, o_is_value=True)
    y_s, conv_s, h_s = _back(o_s, zg_s, x_sample, mod_s[0], mod_s[1],
                             state_conv[0], state_h[0].reshape(bs, 1, w),
                             back_w, nb=nb_s, tt=tt_s, o_is_value=False)

    return (y_p, y_s, ckv_p[None], jnp.swapaxes(kpeT_p, 1, 2)[None], ckv_s[None], kpe_s[None],
            conv_p[None], conv_s[None], h_p.reshape(1, bp, w), h_s.reshape(1, bs, w))
```

```python
import functools
import math

import jax
import jax.numpy as jnp
from jax import lax
from jax.experimental import pallas as pl
from jax.experimental.pallas import tpu as pltpu

F32 = jnp.float32
BF16 = jnp.bfloat16

N_HEADS = 8
QK_NOPE = 128
QK_ROPE = 64
V_DIM = 128
Q_LORA = 512
KV_LORA = 256
QK_CAT = KV_LORA + QK_ROPE
ROPE_THETA = 10000.0
SM_SCALE = 1.0 / math.sqrt(QK_NOPE + QK_ROPE)
Q_SCALE = SM_SCALE * math.log2(math.e)
NEG_INF = -1e30
LRU_BLOCKS = 4
CONV_W = 4
LRU_C = 8.0
NORM_EPS = 1e-6
PAGE_SIZE = 128

VMEM_LIMIT_BYTES = 56 * 1024 * 1024
SUBLANES = 8
ROW_TILE = 256
PROMPT_ROW_TILE = 512
FRONT_ROW_TILE = 1024


def _dot(a, b):
    return jnp.dot(a, b, preferred_element_type=F32)


def _dot_nt(a, b):
    return lax.dot_general(a, b, (((1,), (1,)), ((), ())), preferred_element_type=F32)


def _rms(x, g):
    return x * lax.rsqrt(jnp.mean(x * x, axis=-1, keepdims=True) + NORM_EPS) * g


def _adaln(x, g, scale, shift):
    r = lax.rsqrt(jnp.mean(x * x, axis=-1, keepdims=True) + NORM_EPS)
    return x * r * (g * (1.0 + scale)) + shift


def _silu(x):
    half = 0.5 * x
    return half * jnp.tanh(half) + half


def _params(*sem):
    return pltpu.CompilerParams(dimension_semantics=sem, vmem_limit_bytes=VMEM_LIMIT_BYTES)


def _full(shape):
    return pl.BlockSpec(shape, lambda *_: (0,) * len(shape))


def _mod_kernel(c_ref, w_ref, b_ref, o_ref):
    o_ref[0] = _dot(c_ref[...].astype(BF16), w_ref[0].astype(BF16)) + b_ref[0]


def _modulation(c_all, w_ada, b_ada):
    depth, d, d3 = w_ada.shape
    n = c_all.shape[0]
    return pl.pallas_call(
        _mod_kernel,
        out_shape=jax.ShapeDtypeStruct((depth, n, d3), F32),
        grid=(depth, d3 // d),
        in_specs=[
            pl.BlockSpec((n, d), lambda i, j: (0, 0)),
            pl.BlockSpec((1, d, d), lambda i, j: (i, 0, j)),
            pl.BlockSpec((1, 1, d), lambda i, j: (i, 0, j)),
        ],
        out_specs=pl.BlockSpec((1, n, d), lambda i, j: (i, 0, j)),
        compiler_params=_params("arbitrary", "arbitrary"),
        name="mod",
    )(c_all, w_ada, b_ada.reshape(depth, 1, d3))


def _rope_rows(x, cos, sin_signed, nb, tt):
    width = x.shape[-1]
    lane = lax.broadcasted_iota(jnp.int32, x.shape, 1)
    first_half = (lane & (QK_ROPE - 1)) < (QK_ROPE // 2)
    partner = jnp.where(first_half,
                        pltpu.roll(x, width - QK_ROPE // 2, 1),
                        pltpu.roll(x, QK_ROPE // 2, 1))
    x3 = x.reshape(nb, tt, width)
    p3 = partner.reshape(nb, tt, width)
    return x3 * cos[None] + p3 * sin_signed[None]


def _front_compute(x_ref, shift_ref, scale_ref, cos_ref, sin_ref, g_ref, w_in_ref, gq_ref, w_qb_ref, gkv_ref,
                   nb, tt):
    rows = nb * tt
    d = x_ref.shape[-1]
    h = _adaln(x_ref[...], g_ref[...], scale_ref[...], shift_ref[...])
    hb = h.reshape(rows, d).astype(BF16)
    c0, c1, c2, c3 = Q_LORA, Q_LORA + KV_LORA, Q_LORA + KV_LORA + d, Q_LORA + KV_LORA + d + 2 * QK_ROPE
    u_q = _dot(hb, w_in_ref[:, 0:c0])
    u_kv = _dot(hb, w_in_ref[:, c0:c1])
    u_kpe = _dot(hb, w_in_ref[:, c2:c3])
    zg = lambda: _silu(_dot(hb, w_in_ref[:, c1:c2]))
    ckv = _rms(u_kv, gkv_ref[...])
    cos = cos_ref[...]
    sin = sin_ref[...]
    kr = _rope_rows(u_kpe, cos[:, 0:2 * QK_ROPE], sin[:, 0:2 * QK_ROPE], nb, tt)
    qn = _rms(u_q, gq_ref[...]).astype(BF16)
    q = _dot(qn, w_qb_ref[...])
    return zg, ckv, kr, q, cos, sin


def _front_prompt_kernel(x_ref, shift_ref, scale_ref, cos_ref, sin_ref, g_ref, w_in_ref, gq_ref, w_qb_ref,
                         gkv_ref, w_uk_ref,
                         qT_ref, ckv_ref, kpeT_ref, ckvb_ref, kpeb_ref, ckvT_ref, zg_ref, *, tt, tq):
    nope_w = N_HEADS * QK_NOPE
    pe_w = N_HEADS * QK_ROPE
    zg, ckv, kr, q, cos, sin = _front_compute(
        x_ref, shift_ref, scale_ref, cos_ref, sin_ref, g_ref, w_in_ref, gq_ref, w_qb_ref, gkv_ref, 1, tt)
    ckv_ref[0] = ckv
    ckvb_ref[0] = ckv.astype(ckvb_ref.dtype)
    ckvT = ckv.T.astype(ckvT_ref.dtype)
    kr2 = kr[0]
    kpeb_ref[0] = kr2[:, 0:QK_ROPE].astype(kpeb_ref.dtype)
    kpeT_ref[0] = kr2.T[0:QK_ROPE, :]
    tiles = [(s, slice(s * tq, (s + 1) * tq)) for s in range(tt // tq)]
    for s, tok in tiles:
        ckvT_ref[0, s] = ckvT[:, tok]
    qrT = (_rope_rows(q[:, nope_w:nope_w + pe_w], cos, sin, 1, tt)[0] * Q_SCALE).T
    for hd in range(N_HEADS):
        col = slice(hd * tq, (hd + 1) * tq)
        qh = q[:, hd * QK_NOPE:(hd + 1) * QK_NOPE].astype(BF16)
        qlT = (_dot(qh, w_uk_ref[hd]) * Q_SCALE).T
        for s, tok in tiles:
            qT_ref[0, s, 0:KV_LORA, col] = qlT[:, tok].astype(qT_ref.dtype)
            qT_ref[0, s, KV_LORA:QK_CAT, col] = qrT[hd * QK_ROPE:(hd + 1) * QK_ROPE, tok].astype(qT_ref.dtype)
    zg_ref[0] = zg().astype(zg_ref.dtype)


def _front_sample_kernel(x_ref, shift_ref, scale_ref, cos_ref, sin_ref, g_ref, w_in_ref, gq_ref, w_qb_ref,
                         gkv_ref, w_uk_ref,
                         qlat_ref, qpe_ref, ckv_ref, kpe_ref, zg_ref, *, nb, tt):
    d = x_ref.shape[-1]
    nope_w = N_HEADS * QK_NOPE
    pe_w = N_HEADS * QK_ROPE
    zg, ckv, kr, q, cos, sin = _front_compute(
        x_ref, shift_ref, scale_ref, cos_ref, sin_ref, g_ref, w_in_ref, gq_ref, w_qb_ref, gkv_ref, nb, tt)
    ckv_ref[...] = ckv.reshape(nb, tt, KV_LORA)
    kpe_ref[...] = kr[:, :, 0:QK_ROPE]
    for hd in range(N_HEADS):
        qh = q[:, hd * QK_NOPE:(hd + 1) * QK_NOPE].astype(BF16)
        ql = _dot(qh, w_uk_ref[hd]) * Q_SCALE
        qlat_ref[:, hd] = ql.reshape(nb, tt, KV_LORA)
    qr = _rope_rows(q[:, nope_w:nope_w + pe_w], cos, sin, nb, tt) * Q_SCALE
    for hd in range(N_HEADS):
        qpe_ref[:, hd] = qr[:, :, hd * QK_ROPE:(hd + 1) * QK_ROPE]
    zg_ref[...] = zg().reshape(nb, tt, d).astype(zg_ref.dtype)


def _front_in_specs(nb, tt, d, weights):
    pe_w = N_HEADS * QK_ROPE
    return [
        pl.BlockSpec((nb, tt, d), lambda i, j: (i, j, 0)),
        pl.BlockSpec((nb, 1, d), lambda i, j: (i, 0, 0)),
        pl.BlockSpec((nb, 1, d), lambda i, j: (i, 0, 1)),
        pl.BlockSpec((tt, pe_w), lambda i, j: (j, 0)),
        pl.BlockSpec((tt, pe_w), lambda i, j: (j, 0)),
    ] + [_full(w.shape) for w in weights]


def _front_prompt(x, mod3, cos_t, sin_t, weights, *, tt, tq):
    b, t, d = x.shape
    assert tt % tq == 0 and t % tt == 0
    nq = tt // tq
    nt = t // tq
    out_shape = (
        jax.ShapeDtypeStruct((b, nt, QK_CAT, N_HEADS * tq), BF16),
        jax.ShapeDtypeStruct((b, t, KV_LORA), F32),
        jax.ShapeDtypeStruct((b, QK_ROPE, t), F32),
        jax.ShapeDtypeStruct((b, t, KV_LORA), BF16),
        jax.ShapeDtypeStruct((b, t, QK_ROPE), BF16),
        jax.ShapeDtypeStruct((b, nt, KV_LORA, tq), BF16),
        jax.ShapeDtypeStruct((b, t, d), BF16),
    )
    out_specs = (
        pl.BlockSpec((1, nq, QK_CAT, N_HEADS * tq), lambda i, j: (i, j, 0, 0)),
        pl.BlockSpec((1, tt, KV_LORA), lambda i, j: (i, j, 0)),
        pl.BlockSpec((1, QK_ROPE, tt), lambda i, j: (i, 0, j)),
        pl.BlockSpec((1, tt, KV_LORA), lambda i, j: (i, j, 0)),
        pl.BlockSpec((1, tt, QK_ROPE), lambda i, j: (i, j, 0)),
        pl.BlockSpec((1, nq, KV_LORA, tq), lambda i, j: (i, j, 0, 0)),
        pl.BlockSpec((1, tt, d), lambda i, j: (i, j, 0)),
    )
    return pl.pallas_call(
        functools.partial(_front_prompt_kernel, tt=tt, tq=tq),
        out_shape=out_shape,
        grid=(b, t // tt),
        in_specs=_front_in_specs(1, tt, d, weights),
        out_specs=out_specs,
        compiler_params=_params("arbitrary", "arbitrary"),
        name="front_prompt",
    )(x, mod3, mod3, cos_t, sin_t, *weights)


def _front_sample(x, mod3, cos_t, sin_t, weights, *, nb, tt):
    bseq, t, d = x.shape
    out_shape = (
        jax.ShapeDtypeStruct((bseq, N_HEADS, t, KV_LORA), F32),
        jax.ShapeDtypeStruct((bseq, N_HEADS, t, QK_ROPE), F32),
        jax.ShapeDtypeStruct((bseq, t, KV_LORA), F32),
        jax.ShapeDtypeStruct((bseq, t, QK_ROPE), F32),
        jax.ShapeDtypeStruct((bseq, t, d), BF16),
    )
    out_specs = (
        pl.BlockSpec((nb, N_HEADS, tt, KV_LORA), lambda i, j: (i, 0, j, 0)),
        pl.BlockSpec((nb, N_HEADS, tt, QK_ROPE), lambda i, j: (i, 0, j, 0)),
        pl.BlockSpec((nb, tt, KV_LORA), lambda i, j: (i, j, 0)),
        pl.BlockSpec((nb, tt, QK_ROPE), lambda i, j: (i, j, 0)),
        pl.BlockSpec((nb, tt, d), lambda i, j: (i, j, 0)),
    )
    return pl.pallas_call(
        functools.partial(_front_sample_kernel, nb=nb, tt=tt),
        out_shape=out_shape,
        grid=(bseq // nb, t // tt),
        in_specs=_front_in_specs(nb, tt, d, weights),
        out_specs=out_specs,
        compiler_params=_params("arbitrary", "arbitrary"),
        name="front_sample",
    )(x, mod3, mod3, cos_t, sin_t, *weights)


def _attn_prompt_kernel(qT_ref, k_ref, kp_ref, vT_ref, w_uv_ref, o_ref, m_scr, l_scr, acc_scr, sa_scr, sb_scr,
                        *, tq, nt):
    def scores(qi, j):
        start = pl.multiple_of(j * tq, tq)
        k = k_ref[0, pl.ds(start, tq), :]
        kp = kp_ref[0, pl.ds(start, tq), :]
        return _dot(k, qT_ref[0, qi, 0:KV_LORA, :]) + _dot(kp, qT_ref[0, qi, KV_LORA:QK_CAT, :])

    def accumulate(j, sT, masked):
        if masked:
            kpos = lax.broadcasted_iota(jnp.int32, sT.shape, 0)
            qpos = lax.broadcasted_iota(jnp.int32, sT.shape, 1) & (tq - 1)
            sT = jnp.where(kpos <= qpos, sT, NEG_INF)
        m_prev = m_scr[...]
        m_new = jnp.maximum(m_prev, jnp.max(sT, axis=0, keepdims=True))
        alpha = jnp.exp2(m_prev - m_new)
        pT = jnp.exp2(sT - m_new)
        l_scr[...] = alpha * l_scr[...] + jnp.sum(pT, axis=0, keepdims=True)
        acc_scr[...] = alpha * acc_scr[...] + _dot(vT_ref[0, j], pT.astype(BF16))
        m_scr[...] = m_new

    def query_tile(qi, n_pairs, odd):
        m_scr[...] = jnp.full(m_scr.shape, NEG_INF, F32)
        l_scr[...] = jnp.zeros(l_scr.shape, F32)
        acc_scr[...] = jnp.zeros(acc_scr.shape, F32)

        sa_scr[...] = scores(qi, 0)

        def body(i, carry):
            j = 2 * i
            sb_scr[...] = scores(qi, j + 1)
            accumulate(j, sa_scr[...], False)
            sa_scr[...] = scores(qi, j + 2)
            accumulate(j + 1, sb_scr[...], False)
            return carry

        lax.fori_loop(0, n_pairs, body, 0)
        if odd:
            sb_scr[...] = scores(qi, qi)
            accumulate(qi - 1, sa_scr[...], False)
            accumulate(qi, sb_scr[...], True)
        else:
            accumulate(qi, sa_scr[...], True)

        oT = acc_scr[...] * (1.0 / l_scr[...])
        rows = pl.ds(pl.multiple_of(qi * tq, tq), tq)
        for hd in range(N_HEADS):
            o_h = oT[:, hd * tq:(hd + 1) * tq].T.astype(BF16)
            o_ref[0, rows, hd * V_DIM:(hd + 1) * V_DIM] = _dot(o_h, w_uv_ref[hd]).astype(o_ref.dtype)

    for qi in range(nt):
        query_tile(qi, qi // 2, qi % 2 == 1)


def _attn_prompt(qT, ckvb, kpeb, ckvT, w_uv_p, *, tq):
    b, nt, _, cols = qT.shape
    t = nt * tq
    assert tq & (tq - 1) == 0
    return pl.pallas_call(
        functools.partial(_attn_prompt_kernel, tq=tq, nt=nt),
        out_shape=jax.ShapeDtypeStruct((b, t, N_HEADS * V_DIM), BF16),
        grid=(b,),
        in_specs=[
            pl.BlockSpec((1, nt, QK_CAT, cols), lambda i: (i, 0, 0, 0)),
            pl.BlockSpec((1, t, KV_LORA), lambda i: (i, 0, 0)),
            pl.BlockSpec((1, t, QK_ROPE), lambda i: (i, 0, 0)),
            pl.BlockSpec((1, nt, KV_LORA, tq), lambda i: (i, 0, 0, 0)),
            _full(w_uv_p.shape),
        ],
        out_specs=pl.BlockSpec((1, t, N_HEADS * V_DIM), lambda i: (i, 0, 0)),
        scratch_shapes=[
            pltpu.VMEM((1, cols), F32),
            pltpu.VMEM((1, cols), F32),
            pltpu.VMEM((KV_LORA, cols), F32),
            pltpu.VMEM((tq, cols), F32),
            pltpu.VMEM((tq, cols), F32),
        ],
        compiler_params=_params("arbitrary"),
        name="attn_prompt",
    )(qT, ckvb, kpeb, ckvT, w_uv_p)


SAMPLE_CHUNK_PAGES = 64
SAMPLE_SLOTS = 3


def _sample_stages(pt_ref, ql_ref, qp_ref, kn_ref, kpn_ref, cache_k, cache_pT, o_ref,
                   kbuf, pbuf, kb_scr, sems, *, step, n_steps, spp, n_chunks, cp, ts):
    rows = N_HEADS * ts
    nseq = n_steps * spp
    ahead = SAMPLE_SLOTS - 1

    def page_copies(seq, chunk, slot, page):
        phys = pt_ref[seq, chunk * cp + page]
        dst = pl.ds(page * PAGE_SIZE, PAGE_SIZE)
        return (
            pltpu.make_async_copy(cache_k.at[phys], kbuf.at[slot, dst], sems.at[slot, 0]),
            pltpu.make_async_copy(cache_pT.at[phys], pbuf.at[slot, :, dst], sems.at[slot, 1]),
        )

    def start_chunk(seq, chunk, slot):
        for page in range(cp):
            for thread, c in enumerate(page_copies(seq, chunk, slot, page)):
                c.start(priority=thread)

    def wait_chunk(seq, chunk, slot):
        for page in range(cp):
            for c in page_copies(seq, chunk, slot, page):
                c.wait()

    def update(st, s, v):
        m_new = jnp.maximum(st["m"], jnp.max(s, axis=1, keepdims=True))
        alpha = jnp.exp2(st["m"] - m_new)
        p = jnp.exp2(s - m_new)
        st["l"] = alpha * st["l"] + jnp.sum(p, axis=1, keepdims=True)
        st["acc"] = alpha * st["acc"] + _dot(p.astype(BF16), v)
        st["m"] = m_new

    def make_stage(q, c, st):
        seq = step * spp + q

        def stage():
            if q == 0 and c == 0:
                @pl.when(step == 0)
                def _():
                    for c0 in range(ahead):
                        start_chunk(0, c0, c0 % SAMPLE_SLOTS)
            if c == 0:
                st["ql"] = ql_ref[q].reshape(rows, KV_LORA).astype(BF16)
                st["qp"] = qp_ref[q].reshape(rows, QK_ROPE).astype(BF16)
                st["m"] = jnp.full((rows, 1), NEG_INF, F32)
                st["l"] = jnp.zeros((rows, 1), F32)
                st["acc"] = jnp.zeros((rows, KV_LORA), F32)
            if c < n_chunks:
                slot = lax.rem(seq * n_chunks + c, SAMPLE_SLOTS)
                nxt = c + ahead
                nxt_slot = lax.rem(seq * n_chunks + nxt, SAMPLE_SLOTS)
                if nxt < n_chunks:
                    start_chunk(seq, nxt, nxt_slot)
                else:
                    @pl.when(seq + 1 < nseq)
                    def _():
                        start_chunk(seq + 1, nxt - n_chunks, nxt_slot)
                wait_chunk(seq, c, slot)
                k = kbuf[slot].astype(BF16)
                kb_scr[c % 2] = k
                s_cur = _dot_nt(st["ql"], k) + _dot(st["qp"], pbuf[slot].astype(BF16))
            if c == 0:
                kn = kn_ref[q].astype(BF16)
                kpn = kpn_ref[q].astype(BF16)
                s = _dot_nt(st["ql"], kn) + _dot_nt(st["qp"], kpn)
                s3 = s.reshape(N_HEADS, ts, ts)
                qpos = lax.broadcasted_iota(jnp.int32, s3.shape, 1)
                kpos = lax.broadcasted_iota(jnp.int32, s3.shape, 2)
                update(st, jnp.where(kpos <= qpos, s3, NEG_INF).reshape(rows, ts), kn)
            else:
                update(st, st["s_prev"], kb_scr[(c - 1) % 2])
            if c < n_chunks:
                st["s_prev"] = s_cur
            else:
                o_ref[q] = (st["acc"] / st["l"]).reshape(N_HEADS, ts, KV_LORA)

        return stage

    stages = []
    for q in range(spp):
        st = {}
        stages += [make_stage(q, c, st) for c in range(n_chunks + 1)]
    return stages


def _attn_sample_kernel(pt_ref, ql_ref, qp_ref, kn_ref, kpn_ref, cache_k, cache_pT, o_ref,
                        kbuf, pbuf, kb_scr, sems, *, n_chunks, cp, ts):
    for stage in _sample_stages(pt_ref, ql_ref, qp_ref, kn_ref, kpn_ref, cache_k, cache_pT, o_ref,
                                kbuf, pbuf, kb_scr, sems, step=pl.program_id(0), n_steps=pl.num_programs(0),
                                spp=1, n_chunks=n_chunks, cp=cp, ts=ts):
        stage()


def _attn_sample(page_table, qlat, qpe, ckv_new, kpe_new, cache_ckv, cache_kpeT):
    bs, _, ts, _ = qlat.shape
    n_pages = page_table.shape[1]
    cp = min(SAMPLE_CHUNK_PAGES, n_pages // 2)
    assert n_pages % cp == 0
    n_chunks = n_pages // cp
    assert n_chunks >= SAMPLE_SLOTS - 1
    grid_spec = pltpu.PrefetchScalarGridSpec(
        num_scalar_prefetch=1,
        grid=(bs,),
        in_specs=[
            pl.BlockSpec((1, N_HEADS, ts, KV_LORA), lambda i, pt: (i, 0, 0, 0)),
            pl.BlockSpec((1, N_HEADS, ts, QK_ROPE), lambda i, pt: (i, 0, 0, 0)),
            pl.BlockSpec((1, ts, KV_LORA), lambda i, pt: (i, 0, 0)),
            pl.BlockSpec((1, ts, QK_ROPE), lambda i, pt: (i, 0, 0)),
            pl.BlockSpec(memory_space=pl.ANY),
            pl.BlockSpec(memory_space=pl.ANY),
        ],
        out_specs=pl.BlockSpec((1, N_HEADS, ts, KV_LORA), lambda i, pt: (i, 0, 0, 0)),
        scratch_shapes=[
            pltpu.VMEM((SAMPLE_SLOTS, cp * PAGE_SIZE, KV_LORA), F32),
            pltpu.VMEM((SAMPLE_SLOTS, QK_ROPE, cp * PAGE_SIZE), F32),
            pltpu.VMEM((2, cp * PAGE_SIZE, KV_LORA), BF16),
            pltpu.SemaphoreType.DMA((SAMPLE_SLOTS, 2)),
        ],
    )
    return pl.pallas_call(
        functools.partial(_attn_sample_kernel, n_chunks=n_chunks, cp=cp, ts=ts),
        out_shape=jax.ShapeDtypeStruct((bs, N_HEADS, ts, KV_LORA), F32),
        grid_spec=grid_spec,
        compiler_params=_params("arbitrary"),
        name="attn_sample",
    )(page_table, qlat, qpe, ckv_new, kpe_new, cache_ckv, cache_kpeT)


def _back_phases(o_ref, zg_ref, x_ref, gate0_ref, shift_ref, scale_ref, gate1_ref, conv0_ref, h0_ref,
                 w_uv_ref, w_o_ref, g1_ref, w_in_ref, conv_w_ref, conv_b_ref, w_a_ref, b_a_ref,
                 w_x_ref, b_x_ref, lam_ref, w_ro_ref, gf_ref,
                 y_ref, conv_out_ref, h_out_ref,
                 xprev_scr, hs_scr, hcar_scr, *, nb, tt, o_is_value):
    j = pl.program_id(1)
    rows = nb * tt
    d = x_ref.shape[-1]
    w = lam_ref.shape[-1]
    bw = w // LRU_BLOCKS
    ng = tt // SUBLANES
    st = {}

    def layer0_out():
        zg = zg_ref[...].reshape(rows, d)
        if o_is_value:
            gated = o_ref[...].reshape(rows, d) * zg
        else:
            v = jnp.concatenate(
                [_dot(o_ref[:, hd].reshape(rows, KV_LORA).astype(BF16), w_uv_ref[hd])
                 for hd in range(N_HEADS)], axis=1)
            gated = (v * zg.astype(F32)).astype(BF16)
        out0 = _dot(gated, w_o_ref[...]).reshape(nb, tt, d)
        st["x1"] = x_ref[...] + gate0_ref[...] * out0

    def lru_in():
        h1 = _adaln(st["x1"], g1_ref[...], scale_ref[...], shift_ref[...])
        hb = h1.reshape(rows, d).astype(BF16)
        st["xb"] = _dot(hb, w_in_ref[:, 0:w])
        st["hb"] = hb

    def gate_branch():
        st["zg"] = _silu(_dot(st.pop("hb"), w_in_ref[:, w:2 * w]))

    def init_state():
        @pl.when(j == 0)
        def _():
            xprev_scr[:, 0:SUBLANES - (CONV_W - 1), :] = jnp.zeros((nb, SUBLANES - (CONV_W - 1), w), F32)
            xprev_scr[:, SUBLANES - (CONV_W - 1):SUBLANES, :] = conv0_ref[...]
            hcar_scr[...] = jnp.broadcast_to(h0_ref[...], hcar_scr.shape)

    def conv():
        xb3 = st.pop("xb").reshape(nb, tt, w)
        ext = jnp.concatenate([xprev_scr[...], xb3], axis=1).reshape(nb * (ng + 1), SUBLANES, w)
        xprev_scr[...] = xb3[:, tt - SUBLANES:tt, :]
        conv_out_ref[...] = xb3[:, tt - (CONV_W - 1):tt, :]
        gshape = (nb, ng, SUBLANES, w)
        gpos = lax.broadcasted_iota(jnp.int32, gshape, 2)
        xc = conv_b_ref[...] + xb3.reshape(gshape) * conv_w_ref[CONV_W - 1:CONV_W, :]
        for sh in range(1, CONV_W):
            rolled = pltpu.roll(ext, sh, 1).reshape(nb, ng + 1, SUBLANES, w)
            shifted = jnp.where(gpos >= sh, rolled[:, 1:], rolled[:, :ng])
            xc = xc + shifted * conv_w_ref[CONV_W - 1 - sh:CONV_W - sh, :]
        st["xc"] = xc.reshape(rows, w)

    def gates():
        xc2 = st["xc"]
        xcb = xc2.astype(BF16)
        st["t_r"] = jnp.tanh(jnp.concatenate(
            [_dot(xcb[:, n * bw:(n + 1) * bw], w_a_ref[n]) for n in range(LRU_BLOCKS)], axis=1) + b_a_ref[...])
        st["t_i"] = jnp.tanh(jnp.concatenate(
            [_dot(xcb[:, n * bw:(n + 1) * bw], w_x_ref[n]) for n in range(LRU_BLOCKS)], axis=1) + b_x_ref[...])

    def recurrence_inputs():
        neg_lam = -lam_ref[...]
        softplus = jnp.maximum(neg_lam, 0.0) + jnp.log1p(jnp.exp(-jnp.abs(neg_lam)))
        half_rate = (-0.5 * LRU_C) * softplus
        log_a = half_rate * st.pop("t_r") + half_rate
        a = jnp.exp(log_a)
        one_minus_a2 = -jnp.tanh(log_a) * (a * a + 1.0)
        st["a"] = a
        st["b"] = jnp.sqrt(one_minus_a2) * ((0.5 * st.pop("t_i") + 0.5) * st.pop("xc"))

    def scan_groups():
        a_g = st.pop("a").reshape(nb * ng, SUBLANES, w)
        b_g = st.pop("b").reshape(nb * ng, SUBLANES, w)
        tpos = lax.broadcasted_iota(jnp.int32, a_g.shape, 1)
        for sh in (1, 2, 4):
            keep = tpos >= sh
            a_prev = jnp.where(keep, pltpu.roll(a_g, sh, 1), 1.0)
            b_prev = jnp.where(keep, pltpu.roll(b_g, sh, 1), 0.0)
            b_g = a_g * b_prev + b_g
            a_g = a_g * a_prev
        st["a_g"] = a_g.reshape(nb, ng, SUBLANES, w)
        st["b_g"] = b_g.reshape(nb, ng, SUBLANES, w)

    def scan_carry():
        a_g = st.pop("a_g")
        b_g = st.pop("b_g")
        hprev = hcar_scr[...]
        for g in range(ng):
            hg = a_g[:, g] * hprev + b_g[:, g]
            hs_scr[:, g * SUBLANES:(g + 1) * SUBLANES, :] = hg
            hprev = jnp.broadcast_to(hg[:, SUBLANES - 1:SUBLANES, :], hg.shape)
        hcar_scr[...] = hprev
        h_out_ref[...] = hprev[:, 0:1, :]

    def lru_out():
        hs = hs_scr[...].reshape(rows, w)
        gated1 = (hs * st.pop("zg")).astype(BF16)
        out1 = _dot(gated1, w_ro_ref[...]).reshape(nb, tt, d)
        x2 = st.pop("x1") + gate1_ref[...] * out1
        y_ref[...] = _rms(x2, gf_ref[...])

    return [init_state, layer0_out, lru_in, conv, gates, recurrence_inputs, gate_branch, scan_groups, scan_carry,
            lru_out]


def _back_kernel(*refs, nb, tt, o_is_value):
    for phase in _back_phases(*refs, nb=nb, tt=tt, o_is_value=o_is_value):
        phase()


def _back(o, zg, x, mod0, mod1, conv0, h0, weights, *, nb, tt, o_is_value):
    bseq, t, d = x.shape
    w = conv0.shape[-1]
    if o_is_value:
        o_spec = pl.BlockSpec((nb, tt, d), lambda i, j: (i, j, 0))
    else:
        o_spec = pl.BlockSpec((nb, N_HEADS, tt, KV_LORA), lambda i, j: (i, 0, j, 0))
    in_specs = [
        o_spec,
        pl.BlockSpec((nb, tt, d), lambda i, j: (i, j, 0)),
        pl.BlockSpec((nb, tt, d), lambda i, j: (i, j, 0)),
        pl.BlockSpec((nb, 1, d), lambda i, j: (i, 0, 2)),
        pl.BlockSpec((nb, 1, d), lambda i, j: (i, 0, 0)),
        pl.BlockSpec((nb, 1, d), lambda i, j: (i, 0, 1)),
        pl.BlockSpec((nb, 1, d), lambda i, j: (i, 0, 2)),
        pl.BlockSpec((nb, CONV_W - 1, w), lambda i, j: (i, 0, 0)),
        pl.BlockSpec((nb, 1, w), lambda i, j: (i, 0, 0)),
    ] + [_full(wt.shape) for wt in weights]
    out_shape = (
        jax.ShapeDtypeStruct((bseq, t, d), F32),
        jax.ShapeDtypeStruct((bseq, CONV_W - 1, w), F32),
        jax.ShapeDtypeStruct((bseq, 1, w), F32),
    )
    out_specs = (
        pl.BlockSpec((nb, tt, d), lambda i, j: (i, j, 0)),
        pl.BlockSpec((nb, CONV_W - 1, w), lambda i, j: (i, 0, 0)),
        pl.BlockSpec((nb, 1, w), lambda i, j: (i, 0, 0)),
    )
    return pl.pallas_call(
        functools.partial(_back_kernel, nb=nb, tt=tt, o_is_value=o_is_value),
        out_shape=out_shape,
        grid=(bseq // nb, t // tt),
        in_specs=in_specs,
        out_specs=out_specs,
        scratch_shapes=[
            pltpu.VMEM((nb, SUBLANES, w), F32),
            pltpu.VMEM((nb, tt, w), F32),
            pltpu.VMEM((nb, SUBLANES, w), F32),
        ],
        compiler_params=_params("arbitrary", "arbitrary"),
        name="back_prompt" if o_is_value else "back_sample",
    )(o, zg, x, mod0, mod1, mod1, mod1, conv0, h0, *weights)


def _rope_tables(pos):
    half = QK_ROPE // 2
    inv = ROPE_THETA ** (-jnp.arange(half, dtype=F32) / half)
    ang = pos.astype(F32)[:, None] * inv[None, :]
    cos, sin = jnp.cos(ang), jnp.sin(ang)
    cos_t = jnp.tile(jnp.concatenate([cos, cos], axis=1), (1, N_HEADS))
    sin_t = jnp.tile(jnp.concatenate([-sin, sin], axis=1), (1, N_HEADS))
    return cos_t, sin_t


def _tiles(bseq, t):
    tt = min(ROW_TILE, t)
    nb = max(1, min(bseq, ROW_TILE // tt))
    assert t % tt == 0 and bseq % nb == 0 and tt % SUBLANES == 0
    return nb, tt


def kernel(x_prompt, x_sample, c_prompt, c_sample, cache_ckv, cache_kpe, page_table, state_conv, state_h,
           norm_g, w_ada, b_ada, final_g,
           a_w_in, a_g_q, a_w_qb, a_g_kv, a_w_uk, a_w_uv, a_w_o,
           r_w_in, r_conv_w, r_conv_b, r_w_a, r_b_a, r_w_x, r_b_x, r_lam, r_w_o):
    bp, tp, d = x_prompt.shape
    bs, ts, _ = x_sample.shape
    assert norm_g.shape[0] == 2 and a_w_in.shape[0] == 1 and r_w_in.shape[0] == 1
    w = r_lam.shape[-1]
    past_len = page_table.shape[1] * PAGE_SIZE

    w_in = a_w_in[0]
    c_kv, c_pe = Q_LORA + KV_LORA, Q_LORA + KV_LORA + QK_ROPE
    w_in_p = jnp.concatenate(
        [w_in[:, :c_kv], w_in[:, c_pe:], w_in[:, c_kv:c_pe], w_in[:, c_kv:c_pe]], axis=1).astype(BF16)
    w_qb = a_w_qb[0].reshape(Q_LORA, N_HEADS, QK_NOPE + QK_ROPE)
    w_qb_p = jnp.concatenate(
        [w_qb[:, :, :QK_NOPE].reshape(Q_LORA, N_HEADS * QK_NOPE),
         w_qb[:, :, QK_NOPE:].reshape(Q_LORA, N_HEADS * QK_ROPE)], axis=1).astype(BF16)
    w_uk_p = a_w_uk[0].transpose(1, 2, 0).astype(BF16)
    w_uv_p = a_w_uv[0].transpose(1, 0, 2).astype(BF16)
    row = lambda v: v.reshape(1, -1)
    front_w = (row(norm_g[0]), w_in_p, row(a_g_q[0]), w_qb_p, row(a_g_kv[0]), w_uk_p)
    back_w = (w_uv_p, a_w_o[0].astype(BF16), row(norm_g[1]), r_w_in[0].astype(BF16), r_conv_w[0],
              row(r_conv_b[0]), (0.5 * r_w_a[0]).astype(BF16), row(0.5 * r_b_a[0]),
              (0.5 * r_w_x[0]).astype(BF16), row(0.5 * r_b_x[0]),
              row(r_lam[0]), r_w_o[0].astype(BF16), row(final_g))

    mod = _modulation(jnp.concatenate([c_prompt, c_sample], axis=0), w_ada, b_ada)
    mod_p = [mod[i, :bp].reshape(bp, 1, 3 * d) for i in range(2)]
    mod_s = [mod[i, bp:].reshape(bs, 1, 3 * d) for i in range(2)]

    cos_p, sin_p = _rope_tables(jnp.arange(tp))
    cos_s, sin_s = _rope_tables(past_len + jnp.arange(ts))

    _, tq_p = _tiles(bp, tp)
    tt_p = min(PROMPT_ROW_TILE, tp)
    nb_s, tt_s = _tiles(bs, ts)

    qT_p, ckv_p, kpeT_p, ckvb_p, kpeb_p, ckvT_p, zg_p = _front_prompt(
        x_prompt, mod_p[0], cos_p, sin_p, front_w, tt=min(FRONT_ROW_TILE, tp), tq=tq_p)
    qlat_s, qpe_s, ckv_s, kpe_s, zg_s = _front_sample(
        x_sample, mod_s[0], cos_s, sin_s, front_w, nb=nb_s, tt=tt_s)

    v_p = _attn_prompt(qT_p, ckvb_p, kpeb_p, ckvT_p, w_uv_p, tq=tq_p)

    o_s = _attn_sample(page_table, qlat_s, qpe_s, ckv_s, kpe_s, cache_ckv[0], jnp.swapaxes(cache_kpe[0], 1, 2))

    y_p, conv_p, h_p = _back(v_p, zg_p, x_prompt, mod_p[0], mod_p[1],
                             jnp.zeros((bp, CONV_W - 1, w), F32), jnp.zeros((bp, 1, w), F32),
                             back_w, nb=1, tt=tt_p, o_is_value=True)
    y_s, conv_s, h_s = _back(o_s, zg_s, x_sample, mod_s[0], mod_s[1],
                             state_conv[0], state_h[0].reshape(bs, 1, w),
                             back_w, nb=nb_s, tt=tt_s, o_is_value=False)

    return (y_p, y_s, ckv_p[None], jnp.swapaxes(kpeT_p, 1, 2)[None], ckv_s[None], kpe_s[None],
            conv_p[None], conv_s[None], h_p.reshape(1, bp, w), h_s.reshape(1, bs, w))
```
